```python
import jax, jax.numpy as jnp
from jax import lax
import numpy as np

D_MODEL = 1024
BATCH = 16
SEQ = 4096
DEPTH = 4

D_FF = 2816
MLA_HEADS = 8
Q_LORA = 256
KV_LORA = 128
QK_NOPE = 64
QK_ROPE = 32
V_HEAD = 64
QK_HEAD = QK_NOPE + QK_ROPE
MLA_WIDTH = MLA_HEADS * V_HEAD
ROPE_THETA = 10000.0
Q_BLOCK = 128
POOL_WINDOWS = (2, 4, 8, 16)
POOL_GROUPS = len(POOL_WINDOWS)
POOL_WIDTH = D_MODEL - MLA_WIDTH
POOL_GROUP_DIM = POOL_WIDTH // POOL_GROUPS
EVEN_IN = Q_LORA + KV_LORA + QK_ROPE + POOL_WIDTH
CHUNK = 128
SG_GROUPS = 4
SG_WIDTH = D_MODEL
SG_GROUP_DIM = SG_WIDTH // SG_GROUPS
EPS = 1e-6
N_EVEN = (DEPTH + 1) // 2
N_ODD = DEPTH // 2
MAX_POS_OFFSET = 4096

kernel_name = "hybrid_mla_pool_gmlp_macaron"


def _rmsnorm(x, g):
    xf = x.astype(jnp.float32)
    y = xf * lax.rsqrt(jnp.mean(xf * xf, axis=-1, keepdims=True) + EPS)
    return (y * g.astype(jnp.float32)).astype(x.dtype)


def _swiglu(x, w_gate, w_up, w_down):
    return (jax.nn.silu(x @ w_gate) * (x @ w_up)) @ w_down


def _rope_tables(positions):
    inv_freq = ROPE_THETA ** (-jnp.arange(0, QK_ROPE, 2, dtype=jnp.float32) / QK_ROPE)
    ang = positions.astype(jnp.float32)[..., None] * inv_freq
    return jnp.cos(ang)[:, :, None, :], jnp.sin(ang)[:, :, None, :]


def _rope(x, cos, sin):
    xf = x.astype(jnp.float32)
    x1, x2 = xf[..., : QK_ROPE // 2], xf[..., QK_ROPE // 2:]
    return jnp.concatenate([x1 * cos - x2 * sin, x1 * sin + x2 * cos], axis=-1).astype(x.dtype)


def _causal_attention(q, k, v):
    B, S, H, D = q.shape
    nb = S // Q_BLOCK
    scale = D ** -0.5
    qb = q.reshape(B, nb, Q_BLOCK, H, D).transpose(1, 0, 2, 3, 4)
    k_pos = jnp.arange(S)

    def block(args):
        q_blk, start = args
        s = jnp.einsum('bqhd,bkhd->bhqk', q_blk, k, preferred_element_type=jnp.float32) * scale
        q_pos = start + jnp.arange(Q_BLOCK)
        s = jnp.where(k_pos[None, :] <= q_pos[:, None], s, -jnp.inf)
        p = jax.nn.softmax(s, axis=-1)
        return jnp.einsum('bhqk,bkhd->bqhd', p.astype(v.dtype), v)

    out = lax.map(block, (qb, jnp.arange(nb) * Q_BLOCK))
    return out.transpose(1, 0, 2, 3, 4).reshape(B, S, H, v.shape[-1])


def _multiscale_pool(p, pool_w, pool_scale):
    B, S, _ = p.shape
    pg = p.reshape(B, S, POOL_GROUPS, POOL_GROUP_DIM).astype(jnp.float32)
    cs = jnp.concatenate([jnp.zeros((B, 1, POOL_GROUPS, POOL_GROUP_DIM), jnp.float32),
                          jnp.cumsum(pg, axis=1)], axis=1)
    t = jnp.arange(S)
    outs = []
    for g, w in enumerate(POOL_WINDOWS):
        upper = cs[:, 1:, g]
        lower = jnp.concatenate([jnp.zeros((B, w - 1, POOL_GROUP_DIM), jnp.float32),
                                 cs[:, : S - w + 1, g]], axis=1)
        count = jnp.minimum(t + 1, w).astype(jnp.float32)[None, :, None]
        outs.append((upper - lower) / count - pg[:, :, g])
    pooled = jnp.stack(outs, axis=2).astype(p.dtype)
    mixed = jnp.einsum('bsgc,gcd->bsgd', pooled, pool_w)
    return mixed.reshape(B, S, POOL_WIDTH) * pool_scale


def _mla_pool_mixer(hn, cos, sin, w_in, q_a_g, kv_a_g, w_uq, w_ukv, q_g, k_g,
                    pool_w, pool_scale, w_out):
    B, S, _ = hn.shape
    proj = hn @ w_in
    c_q, c_kv, k_pe, p = jnp.split(
        proj, [Q_LORA, Q_LORA + KV_LORA, Q_LORA + KV_LORA + QK_ROPE], axis=-1)
    q = (_rmsnorm(c_q, q_a_g) @ w_uq).reshape(B, S, MLA_HEADS, QK_HEAD)
    kv = (_rmsnorm(c_kv, kv_a_g) @ w_ukv).reshape(B, S, MLA_HEADS, QK_NOPE + V_HEAD)
    k_nope, v = kv[..., :QK_NOPE], kv[..., QK_NOPE:]
    k = jnp.concatenate(
        [k_nope, jnp.broadcast_to(k_pe[:, :, None, :], (B, S, MLA_HEADS, QK_ROPE))], axis=-1)
    q = _rmsnorm(q, q_g)
    k = _rmsnorm(k, k_g)
    q = jnp.concatenate([q[..., :QK_NOPE], _rope(q[..., QK_NOPE:], cos, sin)], axis=-1)
    k = jnp.concatenate([k[..., :QK_NOPE], _rope(k[..., QK_NOPE:], cos, sin)], axis=-1)
    attn = _causal_attention(q, k, v).reshape(B, S, MLA_WIDTH)
    pooled = _multiscale_pool(p, pool_w, pool_scale)
    return jnp.concatenate([attn, pooled], axis=-1) @ w_out


def _spatial_gating_mixer(hn, w_in, sg_norm_g, sg_w, sg_b, w_out):
    B, S, _ = hn.shape
    uv = jax.nn.gelu(hn @ w_in)
    u, v = jnp.split(uv, 2, axis=-1)
    v = _rmsnorm(v, sg_norm_g)
    vc = v.reshape(B, S // CHUNK, CHUNK, SG_GROUPS, SG_GROUP_DIM)
    w = sg_w * jnp.tril(jnp.ones((CHUNK, CHUNK), sg_w.dtype))
    mixed = jnp.einsum('gts,bnsgc->bntgc', w, vc) + sg_b.T[None, None, :, :, None]
    return (u * mixed.reshape(B, S, SG_WIDTH)) @ w_out


def setup_inputs(seed: int = 0) -> dict:
    key = jax.random.key(seed)
    ks = jax.random.split(key, 24)
    f32 = jnp.float32

    def nrm(k, shape, fan_in):
        return jax.random.normal(k, shape, f32) * (fan_in ** -0.5)

    def gain(k, shape):
        return 1.0 + 0.02 * jax.random.normal(k, shape, f32)

    x = jax.random.normal(ks[0], (BATCH, SEQ, D_MODEL), f32)
    offset = jax.random.randint(ks[1], (BATCH, 1), 0, MAX_POS_OFFSET, dtype=jnp.int32)
    positions = (offset + jnp.arange(SEQ, dtype=jnp.int32)[None, :]).astype(jnp.int32)
    return {
        "x": x,
        "positions": positions,
        "ffn_norm": gain(ks[2], (DEPTH, 2, D_MODEL)),
        "ffn_w_gate": nrm(ks[3], (DEPTH, 2, D_MODEL, D_FF), D_MODEL),
        "ffn_w_up": nrm(ks[4], (DEPTH, 2, D_MODEL, D_FF), D_MODEL),
        "ffn_w_down": nrm(ks[5], (DEPTH, 2, D_FF, D_MODEL), D_FF),
        "mix_norm": gain(ks[6], (DEPTH, D_MODEL)),
        "even_w_in": nrm(ks[7], (N_EVEN, D_MODEL, EVEN_IN), D_MODEL),
        "q_a_norm": gain(ks[8], (N_EVEN, Q_LORA)),
        "kv_a_norm": gain(ks[9], (N_EVEN, KV_LORA)),
        "w_uq": nrm(ks[10], (N_EVEN, Q_LORA, MLA_HEADS * QK_HEAD), Q_LORA),
        "w_ukv": nrm(ks[11], (N_EVEN, KV_LORA, MLA_HEADS * (QK_NOPE + V_HEAD)), KV_LORA),
        "q_norm": gain(ks[12], (N_EVEN, QK_HEAD)),
        "k_norm": gain(ks[13], (N_EVEN, QK_HEAD)),
        "pool_w": nrm(ks[14], (N_EVEN, POOL_GROUPS, POOL_GROUP_DIM, POOL_GROUP_DIM), POOL_GROUP_DIM),
        "pool_scale": gain(ks[15], (N_EVEN, POOL_WIDTH)),
        "even_w_out": nrm(ks[16], (N_EVEN, D_MODEL, D_MODEL), D_MODEL),
        "odd_w_in": nrm(ks[17], (N_ODD, D_MODEL, 2 * SG_WIDTH), D_MODEL),
        "sg_norm": gain(ks[18], (N_ODD, SG_WIDTH)),
        "sg_w": nrm(ks[19], (N_ODD, SG_GROUPS, CHUNK, CHUNK), CHUNK),
        "sg_b": gain(ks[20], (N_ODD, SG_GROUPS, CHUNK)),
        "odd_w_out": nrm(ks[21], (N_ODD, SG_WIDTH, D_MODEL), SG_WIDTH),
    }


def reference(x, positions, ffn_norm, ffn_w_gate, ffn_w_up, ffn_w_down, mix_norm,
              even_w_in, q_a_norm, kv_a_norm, w_uq, w_ukv, q_norm, k_norm,
              pool_w, pool_scale, even_w_out, odd_w_in, sg_norm, sg_w, sg_b, odd_w_out):
    cos, sin = _rope_tables(positions)
    h = x
    for layer in range(DEPTH):
        h = h + 0.5 * _swiglu(_rmsnorm(h, ffn_norm[layer, 0]), ffn_w_gate[layer, 0],
                              ffn_w_up[layer, 0], ffn_w_down[layer, 0])
        hn = _rmsnorm(h, mix_norm[layer])
        i = layer // 2
        if layer % 2 == 0:
            h = h + _mla_pool_mixer(hn, cos, sin, even_w_in[i], q_a_norm[i], kv_a_norm[i],
                                    w_uq[i], w_ukv[i], q_norm[i], k_norm[i],
                                    pool_w[i], pool_scale[i], even_w_out[i])
        else:
            h = h + _spatial_gating_mixer(hn, odd_w_in[i], sg_norm[i], sg_w[i], sg_b[i],
                                          odd_w_out[i])
        h = h + 0.5 * _swiglu(_rmsnorm(h, ffn_norm[layer, 1]), ffn_w_gate[layer, 1],
                              ffn_w_up[layer, 1], ffn_w_down[layer, 1])
    return h
```

```python
import functools

import jax
import jax.numpy as jnp
from jax import lax
from jax.experimental import pallas as pl
from jax.experimental.pallas import tpu as pltpu

F32 = jnp.float32
BF16 = jnp.bfloat16

EPS = 1e-6
MLA_HEADS = 8
Q_LORA = 256
KV_LORA = 128
QK_NOPE = 64
QK_ROPE = 32
V_HEAD = 64
QK_HEAD = QK_NOPE + QK_ROPE
ROPE_THETA = 10000.0
POOL_WINDOWS = (2, 4, 8, 16)
POOL_GROUP_DIM = 128
CHUNK = 128
SG_GROUPS = 4

LANES = 128
HALO = 16
VMEM_LIMIT_BYTES = 56 * 1024 * 1024

FFN_TOKENS = 512
FFN_CHUNK = 256
PROJ_TOKENS = 512
ATTN_Q = 512
ATTN_K = 512
OUT_TOKENS = 512
SG_TOKENS = 512


def _params(n_grid):
    return pltpu.CompilerParams(dimension_semantics=("arbitrary",) * n_grid,
                                vmem_limit_bytes=VMEM_LIMIT_BYTES)


def _const_spec(shape):
    zeros = (0,) * len(shape)
    return pl.BlockSpec(shape, lambda *_: zeros, pipeline_mode=pl.Buffered(1))


def _rms(x, g):
    return x * lax.rsqrt(jnp.mean(x * x, axis=-1, keepdims=True) + EPS) * g


def _ffn_body(x_ref, g_ref, wgu_ref, wd_ref, o_ref, *, n_chunks, tf):
    x = x_ref[...]
    xn = _rms(x, g_ref[...]).astype(BF16)
    acc = None
    for j in range(n_chunks):
        gu = jnp.dot(xn, wgu_ref[j], preferred_element_type=F32)
        g, u = gu[:, :tf], gu[:, tf:]
        h = (g * jax.nn.sigmoid(g) * u).astype(BF16)
        d = jnp.dot(h, wd_ref[j], preferred_element_type=F32)
        acc = d if acc is None else acc + d
    o_ref[...] = x + 0.5 * acc


def _ffn(x, g, wgu, wd):
    n, d = x.shape
    n_chunks, _, tf2 = wgu.shape
    tm = min(FFN_TOKENS, n)
    return pl.pallas_call(
        functools.partial(_ffn_body, n_chunks=n_chunks, tf=tf2 // 2),
        grid=(n // tm,),
        in_specs=[pl.BlockSpec((tm, d), lambda i: (i, 0)),
                  _const_spec(g.shape), _const_spec(wgu.shape), _const_spec(wd.shape)],
        out_specs=pl.BlockSpec((tm, d), lambda i: (i, 0)),
        out_shape=jax.ShapeDtypeStruct((n, d), F32),
        compiler_params=_params(1),
        name="ffn",
    )(x, g, wgu, wd)


def _prep_ffn(w_gate, w_up, w_down):
    d, f = w_gate.shape
    n = f // FFN_CHUNK
    wg = w_gate.reshape(d, n, FFN_CHUNK).transpose(1, 0, 2)
    wu = w_up.reshape(d, n, FFN_CHUNK).transpose(1, 0, 2)
    wgu = jnp.concatenate([wg, wu], axis=-1).astype(BF16)
    wd = w_down.reshape(n, FFN_CHUNK, d).astype(BF16)
    return wgu, wd


def _head_norm_rope(xh, gain, cos, sin_hi, sin_lo):
    ms = jnp.sum(xh * xh, axis=-1, keepdims=True) * (1.0 / QK_HEAD)
    y = xh * lax.rsqrt(ms + EPS) * gain
    return (y * cos + pltpu.roll(y, QK_ROPE // 2, axis=1) * sin_hi
            + pltpu.roll(y, LANES - QK_ROPE // 2, axis=1) * sin_lo)


def _proj_body(h_ref, pos_ref, mixg_ref, win_ref, qag_ref, kvag_ref, wuq_ref, wukv_ref,
               qg_ref, kg_ref, invf_ref, mhi_ref, mlo_ref, q_ref, k_ref, v_ref, p_ref):
    hn = _rms(h_ref[0], mixg_ref[...]).astype(BF16)
    proj = jnp.dot(hn, win_ref[...], preferred_element_type=F32)
    c_q = proj[:, :Q_LORA]
    c_kv = proj[:, Q_LORA:Q_LORA + KV_LORA]
    k_pe = proj[:, Q_LORA + KV_LORA:Q_LORA + KV_LORA + LANES]
    p_ref[0] = proj[:, Q_LORA + KV_LORA + LANES:]

    q = jnp.dot(_rms(c_q, qag_ref[...]).astype(BF16), wuq_ref[...], preferred_element_type=F32)
    kv = jnp.dot(_rms(c_kv, kvag_ref[...]).astype(BF16), wukv_ref[...], preferred_element_type=F32)

    ang = pos_ref[0].astype(F32) * invf_ref[...]
    cos, sin = jnp.cos(ang), jnp.sin(ang)
    sin_hi = sin * mhi_ref[...]
    sin_lo = -(sin * mlo_ref[...])
    for hd in range(MLA_HEADS):
        sl = slice(hd * LANES, (hd + 1) * LANES)
        q_ref[0, :, sl] = _head_norm_rope(q[:, sl], qg_ref[...], cos, sin_hi, sin_lo).astype(BF16)
        k_ref[0, :, sl] = _head_norm_rope(kv[:, sl] + k_pe, kg_ref[...], cos, sin_hi, sin_lo).astype(BF16)
    v_ref[0] = kv[:, MLA_HEADS * LANES:].astype(BF16)


def _proj(h, pos, w):
    b, s, d = h.shape
    tm = min(PROJ_TOKENS, s)
    consts = (w["mix_g"], w["w_in"], w["qa_g"], w["kva_g"], w["w_uq"], w["w_ukv"],
              w["q_g"], w["k_g"], w["invf"], w["m_hi"], w["m_lo"])
    tile = lambda width: pl.BlockSpec((1, tm, width), lambda bi, i: (bi, i, 0))
    hq = MLA_HEADS * LANES
    return pl.pallas_call(
        _proj_body,
        grid=(b, s // tm),
        in_specs=[tile(d), tile(1)] + [_const_spec(c.shape) for c in consts],
        out_specs=[tile(hq), tile(hq), tile(MLA_HEADS * V_HEAD), tile(len(POOL_WINDOWS) * POOL_GROUP_DIM)],
        out_shape=[jax.ShapeDtypeStruct((b, s, hq), BF16),
                   jax.ShapeDtypeStruct((b, s, hq), BF16),
                   jax.ShapeDtypeStruct((b, s, MLA_HEADS * V_HEAD), BF16),
                   jax.ShapeDtypeStruct((b, s, len(POOL_WINDOWS) * POOL_GROUP_DIM), F32)],
        compiler_params=_params(2),
        name="even_proj",
    )(h, pos, *consts)


def _attn_body(q_ref, k_ref, v_ref, o_ref, *, tq, tk):
    i = pl.program_id(2)
    n_sub = tq // tk
    for hh in range(2):
        q = q_ref[0, :, hh * LANES:(hh + 1) * LANES]

        def step(j, carry, masked):
            m, l, acc = carry
            start = pl.multiple_of(j * tk, tk)
            k = k_ref[0, pl.ds(start, tk), hh * LANES:(hh + 1) * LANES]
            v = v_ref[0, pl.ds(start, tk), hh * V_HEAD:(hh + 1) * V_HEAD]
            s = lax.dot_general(q, k, (((1,), (1,)), ((), ())), preferred_element_type=F32)
            if masked:
                row = i * tq + lax.broadcasted_iota(jnp.int32, (tq, tk), 0)
                col = j * tk + lax.broadcasted_iota(jnp.int32, (tq, tk), 1)
                s = jnp.where(col <= row, s, -jnp.inf)
            m_new = jnp.maximum(m, jnp.max(s, axis=-1, keepdims=True))
            alpha = jnp.exp(m - m_new)
            p = jnp.exp(s - m_new)
            l = alpha * l + jnp.sum(p, axis=-1, keepdims=True)
            acc = alpha * acc + jnp.dot(p.astype(BF16), v, preferred_element_type=F32)
            return m_new, l, acc

        init = (jnp.full((tq, 1), -jnp.inf, F32), jnp.zeros((tq, 1), F32), jnp.zeros((tq, V_HEAD), F32))
        carry = lax.fori_loop(0, i * n_sub, functools.partial(step, masked=False), init)
        for d in range(n_sub):
            carry = step(i * n_sub + d, carry, True)
        _, l, acc = carry
        o_ref[0, :, hh * V_HEAD:(hh + 1) * V_HEAD] = (acc / l).astype(o_ref.dtype)


def _attention(q, k, v):
    b, s, _ = q.shape
    tq, tk = min(ATTN_Q, s), min(ATTN_K, s)
    pairs = MLA_HEADS // 2
    return pl.pallas_call(
        functools.partial(_attn_body, tq=tq, tk=tk),
        grid=(b, pairs, s // tq),
        in_specs=[pl.BlockSpec((1, tq, 2 * LANES), lambda bi, hp, i: (bi, i, hp)),
                  pl.BlockSpec((1, s, 2 * LANES), lambda bi, hp, i: (bi, 0, hp)),
                  pl.BlockSpec((1, s, 2 * V_HEAD), lambda bi, hp, i: (bi, 0, hp))],
        out_specs=pl.BlockSpec((1, tq, 2 * V_HEAD), lambda bi, hp, i: (bi, i, hp)),
        out_shape=jax.ShapeDtypeStruct((b, s, MLA_HEADS * V_HEAD), BF16),
        compiler_params=_params(3),
        name="attention",
    )(q, k, v)


def _out_body(a_ref, p_ref, halo_ref, h_ref, pw_ref, ps_ref, woa_ref, wop_ref, o_ref, *, tm):
    i = pl.program_id(1)
    p = p_ref[0]
    halo = jnp.where(i > 0, halo_ref[0], 0.0)
    ext = jnp.concatenate([halo, p], axis=0)
    t = i * tm + lax.broadcasted_iota(jnp.int32, (tm, 1), 0)
    mixed = []
    for g, w in enumerate(POOL_WINDOWS):
        sl = slice(g * POOL_GROUP_DIM, (g + 1) * POOL_GROUP_DIM)
        run, span = ext[:, sl], 1
        while span < w:
            run = run[span:] + run[:-span]
            span *= 2
        first = HALO - (w - 1)
        win = run[first:first + tm]
        count = jnp.minimum(t + 1, w).astype(F32)
        pooled = win / count - p[:, sl]
        mixed.append(jnp.dot(pooled.astype(BF16), pw_ref[g], preferred_element_type=F32))
    mixed = (jnp.concatenate(mixed, axis=-1) * ps_ref[...]).astype(BF16)
    o_ref[0] = (h_ref[0]
                + jnp.dot(a_ref[0], woa_ref[...], preferred_element_type=F32)
                + jnp.dot(mixed, wop_ref[...], preferred_element_type=F32))


def _out(attn, p, h, w):
    b, s, d = h.shape
    tm = min(OUT_TOKENS, s)
    wa, wp = attn.shape[-1], p.shape[-1]
    consts = (w["pool_w"], w["pool_scale"], w["w_out_a"], w["w_out_p"])
    tile = lambda width: pl.BlockSpec((1, tm, width), lambda bi, i: (bi, i, 0))
    halo_spec = pl.BlockSpec((1, HALO, wp), lambda bi, i: (bi, jnp.maximum(i * (tm // HALO) - 1, 0), 0))
    return pl.pallas_call(
        functools.partial(_out_body, tm=tm),
        grid=(b, s // tm),
        in_specs=[tile(wa), tile(wp), halo_spec, tile(d)] + [_const_spec(c.shape) for c in consts],
        out_specs=tile(d),
        out_shape=jax.ShapeDtypeStruct((b, s, d), F32),
        compiler_params=_params(2),
        name="even_out",
    )(attn, p, p, h, *consts)


def _gelu_tanh(x):
    return 0.5 * x * (1.0 + jnp.tanh(0.7978845608028654 * (x + 0.044715 * (x * x * x))))


def _sg_body(h_ref, mixg_ref, win_ref, sgn_ref, sgw_ref, sgb_ref, wout_ref, o_ref, gated_ref, *, tm):
    h = h_ref[...]
    d = h.shape[-1]
    hn = _rms(h, mixg_ref[...]).astype(BF16)
    uv = _gelu_tanh(jnp.dot(hn, win_ref[...], preferred_element_type=F32))
    u = uv[:, :d]
    vn = _rms(uv[:, d:], sgn_ref[...]).astype(BF16)
    gd = d // SG_GROUPS
    causal = (lax.broadcasted_iota(jnp.int32, (CHUNK, CHUNK), 1)
              <= lax.broadcasted_iota(jnp.int32, (CHUNK, CHUNK), 0))
    for g in range(SG_GROUPS):
        wg = jnp.where(causal, sgw_ref[g], 0.0).astype(BF16)
        bias = sgb_ref[:, g:g + 1]
        for c in range(tm // CHUNK):
            rows, cols = slice(c * CHUNK, (c + 1) * CHUNK), slice(g * gd, (g + 1) * gd)
            mixed = jnp.dot(wg, vn[rows, cols], preferred_element_type=F32) + bias
            gated_ref[rows, cols] = (u[rows, cols] * mixed).astype(BF16)
    o_ref[...] = h + jnp.dot(gated_ref[...], wout_ref[...], preferred_element_type=F32)


def _spatial_gating(x, w):
    n, d = x.shape
    tm = min(SG_TOKENS, n)
    consts = (w["mix_g"], w["w_in"], w["sg_norm"], w["sg_w"], w["sg_bt"], w["w_out"])
    return pl.pallas_call(
        functools.partial(_sg_body, tm=tm),
        grid=(n // tm,),
        in_specs=[pl.BlockSpec((tm, d), lambda i: (i, 0))] + [_const_spec(c.shape) for c in consts],
        out_specs=pl.BlockSpec((tm, d), lambda i: (i, 0)),
        out_shape=jax.ShapeDtypeStruct((n, d), F32),
        scratch_shapes=[pltpu.VMEM((tm, d), BF16)],
        compiler_params=_params(1),
        name="spatial_gating",
    )(x, *consts)


def _pad_heads(w, real, lead=0):
    k = w.shape[0]
    w = w.reshape(k, MLA_HEADS, real)
    w = jnp.pad(w, ((0, 0), (0, 0), (lead, LANES - lead - real)))
    return w.reshape(k, MLA_HEADS * LANES)


def _lane_row(values, lead):
    return jnp.pad(values.astype(F32), (lead, LANES - lead - values.shape[0])).reshape(1, LANES)


def _prep_even(mix_g, w_in, qa_g, kva_g, w_uq, w_ukv, q_g, k_g, pool_w, pool_scale, w_out):
    d = w_in.shape[0]
    n_lat = Q_LORA + KV_LORA
    k_pe = jnp.pad(w_in[:, n_lat:n_lat + QK_ROPE], ((0, 0), (QK_NOPE, LANES - QK_HEAD)))
    w_in_p = jnp.concatenate([w_in[:, :n_lat], k_pe, w_in[:, n_lat + QK_ROPE:]], axis=1)
    ukv = w_ukv.reshape(KV_LORA, MLA_HEADS, QK_NOPE + V_HEAD)
    w_k = _pad_heads(ukv[:, :, :QK_NOPE].reshape(KV_LORA, -1), QK_NOPE)
    w_v = ukv[:, :, QK_NOPE:].reshape(KV_LORA, MLA_HEADS * V_HEAD)
    inv_freq = ROPE_THETA ** (-jnp.arange(0, QK_ROPE, 2, dtype=F32) / QK_ROPE)
    half = QK_ROPE // 2
    ones = jnp.ones((half,), F32)
    attn_w = MLA_HEADS * V_HEAD
    return {
        "mix_g": mix_g.reshape(1, d),
        "w_in": w_in_p.astype(BF16),
        "qa_g": qa_g.reshape(1, Q_LORA),
        "kva_g": kva_g.reshape(1, KV_LORA),
        "w_uq": _pad_heads(w_uq, QK_HEAD).astype(BF16),
        "w_ukv": jnp.concatenate([w_k, w_v], axis=1).astype(BF16),
        "q_g": _lane_row(q_g * (QK_HEAD ** -0.5), 0),
        "k_g": _lane_row(k_g, 0),
        "invf": _lane_row(jnp.concatenate([inv_freq, inv_freq]), QK_NOPE),
        "m_hi": _lane_row(ones, QK_NOPE + half),
        "m_lo": _lane_row(ones, QK_NOPE),
        "pool_w": pool_w.astype(BF16),
        "pool_scale": pool_scale.reshape(1, -1),
        "w_out_a": w_out[:attn_w].astype(BF16),
        "w_out_p": w_out[attn_w:].astype(BF16),
    }


def _prep_odd(mix_g, w_in, sg_norm, sg_w, sg_b, w_out):
    d = w_in.shape[0]
    return {
        "mix_g": mix_g.reshape(1, d),
        "w_in": w_in.astype(BF16),
        "sg_norm": sg_norm.reshape(1, -1),
        "sg_w": sg_w,
        "sg_bt": sg_b.T,
        "w_out": w_out.astype(BF16),
    }


def kernel(x, positions, ffn_norm, ffn_w_gate, ffn_w_up, ffn_w_down, mix_norm, even_w_in, q_a_norm, kv_a_norm, w_uq, w_ukv, q_norm, k_norm, pool_w, pool_scale, even_w_out, odd_w_in, sg_norm, sg_w, sg_b, odd_w_out):
    b, s, d = x.shape
    depth = ffn_norm.shape[0]
    pos = positions.reshape(b, s, 1)
    h = x.reshape(b * s, d)
    for layer in range(depth):
        i = layer // 2
        h = _ffn(h, ffn_norm[layer, 0].reshape(1, d),
                 *_prep_ffn(ffn_w_gate[layer, 0], ffn_w_up[layer, 0], ffn_w_down[layer, 0]))
        if layer % 2 == 0:
            w = _prep_even(mix_norm[layer], even_w_in[i], q_a_norm[i], kv_a_norm[i], w_uq[i], w_ukv[i],
                           q_norm[i], k_norm[i], pool_w[i], pool_scale[i], even_w_out[i])
            h3 = h.reshape(b, s, d)
            q, k, v, p = _proj(h3, pos, w)
            attn = _attention(q, k, v)
            h = _out(attn, p, h3, w).reshape(b * s, d)
        else:
            w = _prep_odd(mix_norm[layer], odd_w_in[i], sg_norm[i], sg_w[i], sg_b[i], odd_w_out[i])
            h = _spatial_gating(h, w)
        h = _ffn(h, ffn_norm[layer, 1].reshape(1, d),
                 *_prep_ffn(ffn_w_gate[layer, 1], ffn_w_up[layer, 1], ffn_w_down[layer, 1]))
    return h.reshape(b, s, d)
```

```python
import functools

import jax
import jax.numpy as jnp
from jax import lax
from jax.experimental import pallas as pl
from jax.experimental.pallas import tpu as pltpu

F32 = jnp.float32
BF16 = jnp.bfloat16

EPS = 1e-6
MLA_HEADS = 8
Q_LORA = 256
KV_LORA = 128
QK_NOPE = 64
QK_ROPE = 32
V_HEAD = 64
QK_HEAD = QK_NOPE + QK_ROPE
ROPE_THETA = 10000.0
LOG2_E = 1.4426950408889634
POOL_WINDOWS = (2, 4, 8, 16)
POOL_GROUP_DIM = 128
CHUNK = 128
SG_GROUPS = 4

LANES = 128
HALO = 16
VMEM_LIMIT_BYTES = 56 * 1024 * 1024

FFN_TOKENS = 512
FFN_CHUNK = 256
PROJ_TOKENS = 512
ATTN_Q = 512
ATTN_K = 2048
ATTN_HEADS = 2
OUT_TOKENS = 512
SG_TOKENS = 512


def _params(n_grid, flags=None):
    return pltpu.CompilerParams(dimension_semantics=("arbitrary",) * n_grid,
                                vmem_limit_bytes=VMEM_LIMIT_BYTES, flags=flags)


def _const_spec(shape):
    zeros = (0,) * len(shape)
    return pl.BlockSpec(shape, lambda *_: zeros, pipeline_mode=pl.Buffered(1))


def _rms(x, g):
    return x * lax.rsqrt(jnp.mean(x * x, axis=-1, keepdims=True) + EPS) * g


def _ffn_body(x_ref, g_ref, wg_ref, wu_ref, wd_ref, o_ref, *, tf):
    x = x_ref[...]
    xn = _rms(x, g_ref[...]).astype(BF16)
    acc = None
    for j in range(wg_ref.shape[1] // tf):
        cols = slice(j * tf, (j + 1) * tf)
        g = jnp.dot(xn, wg_ref[:, cols], preferred_element_type=F32)
        u = jnp.dot(xn, wu_ref[:, cols], preferred_element_type=F32)
        h = (g * jax.nn.sigmoid(g) * u).astype(BF16)
        d = jnp.dot(h, wd_ref[cols, :], preferred_element_type=F32)
        acc = d if acc is None else acc + d
    o_ref[...] = x + 0.5 * acc


def _ffn(x, g, wg, wu, wd, layer, which):
    n, d = x.shape
    f = wg.shape[-1]
    assert f % FFN_CHUNK == 0
    tm = min(FFN_TOKENS, n)
    pick = lambda rows, cols: pl.BlockSpec((None, None, rows, cols), lambda i: (layer, which, 0, 0),
                                           pipeline_mode=pl.Buffered(1))
    return pl.pallas_call(
        functools.partial(_ffn_body, tf=FFN_CHUNK),
        grid=(n // tm,),
        in_specs=[pl.BlockSpec((tm, d), lambda i: (i, 0)),
                  pl.BlockSpec((None, 1, d), lambda i: (2 * layer + which, 0, 0), pipeline_mode=pl.Buffered(1)),
                  pick(d, f), pick(d, f), pick(f, d)],
        out_specs=pl.BlockSpec((tm, d), lambda i: (i, 0)),
        out_shape=jax.ShapeDtypeStruct((n, d), F32),
        compiler_params=_params(1),
        name="ffn",
    )(x, g, wg, wu, wd)


def _head_inv_rms(xh):
    return lax.rsqrt(jnp.sum(xh * xh, axis=-1, keepdims=True) * (1.0 / QK_HEAD) + EPS)


def _proj_body(h_ref, pos_ref, mixg_ref, win_ref, qag_ref, kvag_ref, wuq_ref, wukv_ref,
               qg_ref, qgs_ref, kg_ref, kgs_ref, invf_ref, q_ref, k_ref, v_ref, p_ref):
    hq = MLA_HEADS * LANES
    hn = _rms(h_ref[0], mixg_ref[...]).astype(BF16)
    proj = jnp.dot(hn, win_ref[...], preferred_element_type=F32)
    n_lat = Q_LORA + KV_LORA
    c_q = proj[:, :Q_LORA]
    c_kv = proj[:, Q_LORA:n_lat]
    k_pe = proj[:, n_lat:n_lat + LANES]
    k_pe_swap = proj[:, n_lat + LANES:n_lat + 2 * LANES]
    p_ref[0] = proj[:, n_lat + 2 * LANES:]

    q = jnp.dot(_rms(c_q, qag_ref[...]).astype(BF16), wuq_ref[...], preferred_element_type=F32)
    kv = jnp.dot(_rms(c_kv, kvag_ref[...]).astype(BF16), wukv_ref[...], preferred_element_type=F32)

    ang = pos_ref[0].astype(F32) * invf_ref[...]
    cos, sin = jnp.cos(ang), jnp.sin(ang)
    q_cos, q_sin = qg_ref[...] * cos, qgs_ref[...] * sin
    k_cos, k_sin = kg_ref[...] * cos, kgs_ref[...] * sin
    k_rot = k_pe_swap * k_sin
    for hd in range(MLA_HEADS):
        sl = slice(hd * LANES, (hd + 1) * LANES)
        qh = q[:, sl]
        qs = q[:, hq + hd * LANES:hq + (hd + 1) * LANES]
        q_ref[0, :, sl] = ((qh * q_cos + qs * q_sin) * _head_inv_rms(qh)).astype(BF16)
        kh = kv[:, sl] + k_pe
        k_ref[0, :, sl] = ((kh * k_cos + k_rot) * _head_inv_rms(kh)).astype(BF16)
    v_ref[0] = kv[:, hq:].astype(BF16)


def _proj(h, pos, w):
    b, s, d = h.shape
    tm = min(PROJ_TOKENS, s)
    consts = (w["mix_g"], w["w_in"], w["qa_g"], w["kva_g"], w["w_uq"], w["w_ukv"],
              w["q_g"], w["q_gs"], w["k_g"], w["k_gs"], w["invf"])
    tile = lambda width: pl.BlockSpec((1, tm, width), lambda bi, i: (bi, i, 0))
    hq = MLA_HEADS * LANES
    return pl.pallas_call(
        _proj_body,
        grid=(b, s // tm),
        in_specs=[tile(d), tile(1)] + [_const_spec(c.shape) for c in consts],
        out_specs=[tile(hq), tile(hq), tile(MLA_HEADS * V_HEAD), tile(len(POOL_WINDOWS) * POOL_GROUP_DIM)],
        out_shape=[jax.ShapeDtypeStruct((b, s, hq), BF16),
                   jax.ShapeDtypeStruct((b, s, hq), BF16),
                   jax.ShapeDtypeStruct((b, s, MLA_HEADS * V_HEAD), BF16),
                   jax.ShapeDtypeStruct((b, s, len(POOL_WINDOWS) * POOL_GROUP_DIM), F32)],
        compiler_params=_params(2),
        name="even_proj",
    )(h, pos, *consts)


def _attn_tile(q_ref, k_ref, v_ref, o_ref, *, tq, tk, heads, n_blocks):
    width = n_blocks * tq
    bounds = [(c, min(c + tk, width - tq)) for c in range(0, width - tq, tk)] + [(width - tq, width)]
    row = lax.broadcasted_iota(jnp.int32, (tq, tq), 0)
    col = lax.broadcasted_iota(jnp.int32, (tq, tq), 1)
    for hh in range(heads):
        lanes = slice(hh * LANES, (hh + 1) * LANES)
        vcols = slice(hh * V_HEAD, (hh + 1) * V_HEAD)
        q = q_ref[0, :, lanes]
        scores = [lax.dot_general(q, k_ref[0, lo:hi, lanes], (((1,), (1,)), ((), ())),
                                  preferred_element_type=F32) for lo, hi in bounds]
        scores[-1] = jnp.where(col <= row, scores[-1], -jnp.inf)
        m = functools.reduce(jnp.maximum, [jnp.max(s, axis=-1, keepdims=True) for s in scores])
        l, acc = None, None
        for s, (lo, hi) in zip(scores, bounds):
            p = jnp.exp2(s - m)
            ps = jnp.sum(p, axis=-1, keepdims=True)
            pv = jnp.dot(p.astype(BF16), v_ref[0, lo:hi, vcols], preferred_element_type=F32)
            l, acc = (ps, pv) if l is None else (l + ps, acc + pv)
        o_ref[0, :, vcols] = (acc * (1.0 / l)).astype(o_ref.dtype)


def _attn_body(q_ref, k_ref, v_ref, o_ref, *, tq, tk, heads):
    i = pl.program_id(2)
    for n in range(k_ref.shape[1] // tq):
        pl.when(i == n)(functools.partial(_attn_tile, q_ref, k_ref, v_ref, o_ref,
                                          tq=tq, tk=tk, heads=heads, n_blocks=n + 1))


def _attention(q, k, v):
    b, s, _ = q.shape
    tq, tk, hg = min(ATTN_Q, s), min(ATTN_K, s), ATTN_HEADS
    assert s % tq == 0 and tk % tq == 0 and MLA_HEADS % hg == 0
    return pl.pallas_call(
        functools.partial(_attn_body, tq=tq, tk=tk, heads=hg),
        grid=(b, MLA_HEADS // hg, s // tq),
        in_specs=[pl.BlockSpec((1, tq, hg * LANES), lambda bi, g, i: (bi, i, g)),
                  pl.BlockSpec((1, s, hg * LANES), lambda bi, g, i: (bi, 0, g)),
                  pl.BlockSpec((1, s, hg * V_HEAD), lambda bi, g, i: (bi, 0, g))],
        out_specs=pl.BlockSpec((1, tq, hg * V_HEAD), lambda bi, g, i: (bi, i, g)),
        out_shape=jax.ShapeDtypeStruct((b, s, MLA_HEADS * V_HEAD), BF16),
        compiler_params=_params(3),
        name="attention",
    )(q, k, v)


def _out_body(a_ref, p_ref, halo_ref, h_ref, pw_ref, ps_ref, woa_ref, wop_ref, o_ref, *, tm):
    i = pl.program_id(1)
    p = p_ref[0]
    halo = jnp.where(i > 0, halo_ref[0], 0.0)
    ext = jnp.concatenate([halo, p], axis=0)
    t = i * tm + lax.broadcasted_iota(jnp.int32, (tm, 1), 0)
    mixed = []
    for g, w in enumerate(POOL_WINDOWS):
        sl = slice(g * POOL_GROUP_DIM, (g + 1) * POOL_GROUP_DIM)
        run, span = ext[:, sl], 1
        while span < w:
            run = run[span:] + run[:-span]
            span *= 2
        first = HALO - (w - 1)
        win = run[first:first + tm]
        count = jnp.minimum(t + 1, w).astype(F32)
        pooled = win / count - p[:, sl]
        mixed.append(jnp.dot(pooled.astype(BF16), pw_ref[g], preferred_element_type=F32))
    mixed = (jnp.concatenate(mixed, axis=-1) * ps_ref[...]).astype(BF16)
    o_ref[0] = (h_ref[0]
                + jnp.dot(a_ref[0], woa_ref[...], preferred_element_type=F32)
                + jnp.dot(mixed, wop_ref[...], preferred_element_type=F32))


def _out(attn, p, h, w):
    b, s, d = h.shape
    tm = min(OUT_TOKENS, s)
    wa, wp = attn.shape[-1], p.shape[-1]
    consts = (w["pool_w"], w["pool_scale"], w["w_out_a"], w["w_out_p"])
    tile = lambda width: pl.BlockSpec((1, tm, width), lambda bi, i: (bi, i, 0))
    halo_spec = pl.BlockSpec((1, HALO, wp), lambda bi, i: (bi, jnp.maximum(i * (tm // HALO) - 1, 0), 0))
    return pl.pallas_call(
        functools.partial(_out_body, tm=tm),
        grid=(b, s // tm),
        in_specs=[tile(wa), tile(wp), halo_spec, tile(d)] + [_const_spec(c.shape) for c in consts],
        out_specs=tile(d),
        out_shape=jax.ShapeDtypeStruct((b, s, d), F32),
        compiler_params=_params(2),
        name="even_out",
    )(attn, p, p, h, *consts)


def _gelu_tanh(x):
    return 0.5 * x * (1.0 + jnp.tanh(0.7978845608028654 * (x + 0.044715 * (x * x * x))))


def _sg_body(h_ref, mixg_ref, win_ref, sgn_ref, sgw_ref, sgb_ref, wout_ref, o_ref, gated_ref, *, tm):
    h = h_ref[...]
    d = h.shape[-1]
    hn = _rms(h, mixg_ref[...]).astype(BF16)
    uv = _gelu_tanh(jnp.dot(hn, win_ref[...], preferred_element_type=F32))
    u = uv[:, :d]
    vn = _rms(uv[:, d:], sgn_ref[...]).astype(BF16)
    gd = d // SG_GROUPS
    causal = (lax.broadcasted_iota(jnp.int32, (CHUNK, CHUNK), 1)
              <= lax.broadcasted_iota(jnp.int32, (CHUNK, CHUNK), 0))
    for g in range(SG_GROUPS):
        wg = jnp.where(causal, sgw_ref[g], 0.0).astype(BF16)
        bias = sgb_ref[:, g:g + 1]
        for c in range(tm // CHUNK):
            rows, cols = slice(c * CHUNK, (c + 1) * CHUNK), slice(g * gd, (g + 1) * gd)
            mixed = jnp.dot(wg, vn[rows, cols], preferred_element_type=F32) + bias
            gated_ref[rows, cols] = (u[rows, cols] * mixed).astype(BF16)
    o_ref[...] = h + jnp.dot(gated_ref[...], wout_ref[...], preferred_element_type=F32)


def _spatial_gating(x, w):
    n, d = x.shape
    tm = min(SG_TOKENS, n)
    consts = (w["mix_g"], w["w_in"], w["sg_norm"], w["sg_w"], w["sg_bt"], w["w_out"])
    return pl.pallas_call(
        functools.partial(_sg_body, tm=tm),
        grid=(n // tm,),
        in_specs=[pl.BlockSpec((tm, d), lambda i: (i, 0))] + [_const_spec(c.shape) for c in consts],
        out_specs=pl.BlockSpec((tm, d), lambda i: (i, 0)),
        out_shape=jax.ShapeDtypeStruct((n, d), F32),
        scratch_shapes=[pltpu.VMEM((tm, d), BF16)],
        compiler_params=_params(1),
        name="spatial_gating",
    )(x, *consts)


def _pad_heads(w, real, lead=0):
    k = w.shape[0]
    w = w.reshape(k, MLA_HEADS, real)
    w = jnp.pad(w, ((0, 0), (0, 0), (lead, LANES - lead - real)))
    return w.reshape(k, MLA_HEADS * LANES)


def _lane_row(values, lead):
    return jnp.pad(values.astype(F32), (lead, LANES - lead - values.shape[0])).reshape(1, LANES)


def _swap_halves(rope):
    half = QK_ROPE // 2
    return jnp.concatenate([rope[..., half:], rope[..., :half]], axis=-1)


def _signed_swapped_gain(g):
    half = QK_ROPE // 2
    rope = g[QK_NOPE:]
    return _lane_row(jnp.concatenate([-rope[half:], rope[:half]]), QK_NOPE)


def _prep_even(mix_g, w_in, qa_g, kva_g, w_uq, w_ukv, q_g, k_g, pool_w, pool_scale, w_out):
    d = w_in.shape[0]
    n_lat = Q_LORA + KV_LORA
    rope_pad = ((0, 0), (QK_NOPE, LANES - QK_HEAD))
    k_pe_cols = w_in[:, n_lat:n_lat + QK_ROPE]
    w_in_p = jnp.concatenate([w_in[:, :n_lat], jnp.pad(k_pe_cols, rope_pad),
                              jnp.pad(_swap_halves(k_pe_cols), rope_pad), w_in[:, n_lat + QK_ROPE:]], axis=1)
    uq = w_uq.reshape(Q_LORA, MLA_HEADS, QK_HEAD)
    uq_swap = jnp.pad(_swap_halves(uq[:, :, QK_NOPE:]), ((0, 0), (0, 0), (QK_NOPE, LANES - QK_HEAD)))
    ukv = w_ukv.reshape(KV_LORA, MLA_HEADS, QK_NOPE + V_HEAD)
    w_k = _pad_heads(ukv[:, :, :QK_NOPE].reshape(KV_LORA, -1), QK_NOPE)
    w_v = ukv[:, :, QK_NOPE:].reshape(KV_LORA, MLA_HEADS * V_HEAD)
    inv_freq = ROPE_THETA ** (-jnp.arange(0, QK_ROPE, 2, dtype=F32) / QK_ROPE)
    attn_w = MLA_HEADS * V_HEAD
    q_gain = q_g * (QK_HEAD ** -0.5 * LOG2_E)
    return {
        "mix_g": mix_g.reshape(1, d),
        "w_in": w_in_p.astype(BF16),
        "qa_g": qa_g.reshape(1, Q_LORA),
        "kva_g": kva_g.reshape(1, KV_LORA),
        "w_uq": jnp.concatenate([_pad_heads(w_uq, QK_HEAD),
                                 uq_swap.reshape(Q_LORA, MLA_HEADS * LANES)], axis=1).astype(BF16),
        "w_ukv": jnp.concatenate([w_k, w_v], axis=1).astype(BF16),
        "q_g": _lane_row(q_gain, 0),
        "q_gs": _signed_swapped_gain(q_gain),
        "k_g": _lane_row(k_g, 0),
        "k_gs": _signed_swapped_gain(k_g),
        "invf": _lane_row(jnp.concatenate([inv_freq, inv_freq]), QK_NOPE),
        "pool_w": pool_w.astype(BF16),
        "pool_scale": pool_scale.reshape(1, -1),
        "w_out_a": w_out[:attn_w].astype(BF16),
        "w_out_p": w_out[attn_w:].astype(BF16),
    }


def _prep_odd(mix_g, w_in, sg_norm, sg_w, sg_b, w_out):
    d = w_in.shape[0]
    return {
        "mix_g": mix_g.reshape(1, d),
        "w_in": w_in.astype(BF16),
        "sg_norm": sg_norm.reshape(1, -1),
        "sg_w": sg_w,
        "sg_bt": sg_b.T,
        "w_out": w_out.astype(BF16),
    }


def kernel(x, positions, ffn_norm, ffn_w_gate, ffn_w_up, ffn_w_down, mix_norm, even_w_in, q_a_norm, kv_a_norm, w_uq, w_ukv, q_norm, k_norm, pool_w, pool_scale, even_w_out, odd_w_in, sg_norm, sg_w, sg_b, odd_w_out):
    b, s, d = x.shape
    depth = ffn_norm.shape[0]
    pos = positions.reshape(b, s, 1)
    h = x.reshape(b * s, d)
    wg, wu, wd = ffn_w_gate.astype(BF16), ffn_w_up.astype(BF16), ffn_w_down.astype(BF16)
    fg = ffn_norm.reshape(depth * 2, 1, d)
    for layer in range(depth):
        i = layer // 2
        h = _ffn(h, fg, wg, wu, wd, layer, 0)
        if layer % 2 == 0:
            w = _prep_even(mix_norm[layer], even_w_in[i], q_a_norm[i], kv_a_norm[i], w_uq[i], w_ukv[i],
                           q_norm[i], k_norm[i], pool_w[i], pool_scale[i], even_w_out[i])
            h3 = h.reshape(b, s, d)
            q, k, v, p = _proj(h3, pos, w)
            attn = _attention(q, k, v)
            h = _out(attn, p, h3, w).reshape(b * s, d)
        else:
            w = _prep_odd(mix_norm[layer], odd_w_in[i], sg_norm[i], sg_w[i], sg_b[i], odd_w_out[i])
            h = _spatial_gating(h, w)
        h = _ffn(h, fg, wg, wu, wd, layer, 1)
    return h.reshape(b, s, d)
```

```python
import functools

import jax
import jax.numpy as jnp
from jax import lax
from jax.experimental import pallas as pl
from jax.experimental.pallas import tpu as pltpu

F32 = jnp.float32
BF16 = jnp.bfloat16

EPS = 1e-6
MLA_HEADS = 8
Q_LORA = 256
KV_LORA = 128
QK_NOPE = 64
QK_ROPE = 32
V_HEAD = 64
QK_HEAD = QK_NOPE + QK_ROPE
ROPE_THETA = 10000.0
LOG2_E = 1.4426950408889634
POOL_WINDOWS = (2, 4, 8, 16)
POOL_GROUP_DIM = 128
CHUNK = 128
SG_GROUPS = 4

LANES = 128
HALO = 16
VMEM_LIMIT_BYTES = 56 * 1024 * 1024

FFN_TOKENS = 512
FFN_CHUNK = 256
PROJ_TOKENS = 512
ATTN_Q = 512
ATTN_K = 2048
ATTN_HEADS = 4
OUT_TOKENS = 512
SG_TOKENS = 512


def _params(n_grid, flags=None):
    return pltpu.CompilerParams(dimension_semantics=("arbitrary",) * n_grid,
                                vmem_limit_bytes=VMEM_LIMIT_BYTES, flags=flags)


def _const_spec(shape):
    zeros = (0,) * len(shape)
    return pl.BlockSpec(shape, lambda *_: zeros, pipeline_mode=pl.Buffered(1))


def _rms(x, g):
    return x * lax.rsqrt(jnp.mean(x * x, axis=-1, keepdims=True) + EPS) * g


def _ffn_body(x_ref, g_ref, wg_ref, wu_ref, wd_ref, o_ref, *, tf):
    x = x_ref[...]
    xn = _rms(x, g_ref[...]).astype(BF16)
    acc = None
    for j in range(wg_ref.shape[1] // tf):
        cols = slice(j * tf, (j + 1) * tf)
        g = jnp.dot(xn, wg_ref[:, cols], preferred_element_type=F32)
        u = jnp.dot(xn, wu_ref[:, cols], preferred_element_type=F32)
        h = (g * jax.nn.sigmoid(g) * u).astype(BF16)
        d = jnp.dot(h, wd_ref[cols, :], preferred_element_type=F32)
        acc = d if acc is None else acc + d
    o_ref[...] = x + 0.5 * acc


def _ffn(x, g, wg, wu, wd, layer, which):
    n, d = x.shape
    f = wg.shape[-1]
    assert f % FFN_CHUNK == 0
    tm = min(FFN_TOKENS, n)
    pick = lambda rows, cols: pl.BlockSpec((None, None, rows, cols), lambda i: (layer, which, 0, 0),
                                           pipeline_mode=pl.Buffered(1))
    return pl.pallas_call(
        functools.partial(_ffn_body, tf=FFN_CHUNK),
        grid=(n // tm,),
        in_specs=[pl.BlockSpec((tm, d), lambda i: (i, 0)),
                  pl.BlockSpec((None, 1, d), lambda i: (2 * layer + which, 0, 0), pipeline_mode=pl.Buffered(1)),
                  pick(d, f), pick(d, f), pick(f, d)],
        out_specs=pl.BlockSpec((tm, d), lambda i: (i, 0)),
        out_shape=jax.ShapeDtypeStruct((n, d), F32),
        compiler_params=_params(1),
        name="ffn",
    )(x, g, wg, wu, wd)


def _head_inv_rms(xh):
    return lax.rsqrt(jnp.sum(xh * xh, axis=-1, keepdims=True) * (1.0 / QK_HEAD) + EPS)


def _proj_body(h_ref, pos_ref, mixg_ref, win_ref, qag_ref, kvag_ref, wuq_ref, wukv_ref,
               qg_ref, qgs_ref, kg_ref, kgs_ref, invf_ref, vone_ref, q_ref, k_ref, v_ref, p_ref):
    hq = MLA_HEADS * LANES
    hn = _rms(h_ref[0], mixg_ref[...]).astype(BF16)
    proj = jnp.dot(hn, win_ref[...], preferred_element_type=F32)
    n_lat = Q_LORA + KV_LORA
    c_q = proj[:, :Q_LORA]
    c_kv = proj[:, Q_LORA:n_lat]
    k_pe = proj[:, n_lat:n_lat + LANES]
    k_pe_swap = proj[:, n_lat + LANES:n_lat + 2 * LANES]
    p_ref[0] = proj[:, n_lat + 2 * LANES:]

    q = jnp.dot(_rms(c_q, qag_ref[...]).astype(BF16), wuq_ref[...], preferred_element_type=F32)
    kv = jnp.dot(_rms(c_kv, kvag_ref[...]).astype(BF16), wukv_ref[...], preferred_element_type=F32)

    ang = pos_ref[0].astype(F32) * invf_ref[...]
    cos, sin = jnp.cos(ang), jnp.sin(ang)
    q_cos, q_sin = qg_ref[...] * cos, qgs_ref[...] * sin
    k_cos, k_sin = kg_ref[...] * cos, kgs_ref[...] * sin
    k_rot = k_pe_swap * k_sin
    for hd in range(MLA_HEADS):
        sl = slice(hd * LANES, (hd + 1) * LANES)
        qh = q[:, sl]
        qs = q[:, hq + hd * LANES:hq + (hd + 1) * LANES]
        q_ref[0, :, sl] = ((qh * q_cos + qs * q_sin) * _head_inv_rms(qh)).astype(BF16)
        kh = kv[:, sl] + k_pe
        k_ref[0, :, sl] = ((kh * k_cos + k_rot) * _head_inv_rms(kh)).astype(BF16)
    v_ref[0] = (kv[:, hq:] + vone_ref[...]).astype(BF16)


def _proj(h, pos, w):
    b, s, d = h.shape
    tm = min(PROJ_TOKENS, s)
    consts = (w["mix_g"], w["w_in"], w["qa_g"], w["kva_g"], w["w_uq"], w["w_ukv"],
              w["q_g"], w["q_gs"], w["k_g"], w["k_gs"], w["invf"], w["v_one"])
    tile = lambda width: pl.BlockSpec((1, tm, width), lambda bi, i: (bi, i, 0))
    hq = MLA_HEADS * LANES
    return pl.pallas_call(
        _proj_body,
        grid=(b, s // tm),
        in_specs=[tile(d), tile(1)] + [_const_spec(c.shape) for c in consts],
        out_specs=[tile(hq), tile(hq), tile(hq), tile(len(POOL_WINDOWS) * POOL_GROUP_DIM)],
        out_shape=[jax.ShapeDtypeStruct((b, s, hq), BF16),
                   jax.ShapeDtypeStruct((b, s, hq), BF16),
                   jax.ShapeDtypeStruct((b, s, hq), BF16),
                   jax.ShapeDtypeStruct((b, s, len(POOL_WINDOWS) * POOL_GROUP_DIM), F32)],
        compiler_params=_params(2),
        name="even_proj",
    )(h, pos, *consts)


def _attn_tile(q_ref, k_ref, v_ref, o_ref, *, tq, tk, heads, n_blocks):
    width = n_blocks * tq
    bounds = [(c, min(c + tk, width - tq)) for c in range(0, width - tq, tk)] + [(width - tq, width)]
    row = lax.broadcasted_iota(jnp.int32, (tq, tq), 0)
    col = lax.broadcasted_iota(jnp.int32, (tq, tq), 1)
    for hh in range(heads):
        lanes = slice(hh * LANES, (hh + 1) * LANES)
        vcols = slice(hh * V_HEAD, (hh + 1) * V_HEAD)
        q = q_ref[0, :, lanes]
        scores = [lax.dot_general(q, k_ref[0, lo:hi, lanes], (((1,), (1,)), ((), ())),
                                  preferred_element_type=F32) for lo, hi in bounds]
        scores[-1] = jnp.where(col <= row, scores[-1], -jnp.inf)
        m = functools.reduce(jnp.maximum, [jnp.max(s, axis=-1, keepdims=True) for s in scores])
        acc = None
        for s, (lo, hi) in zip(scores, bounds):
            p = jnp.exp2(s - m).astype(BF16)
            pv = jnp.dot(p, v_ref[0, lo:hi, lanes], preferred_element_type=F32)
            acc = pv if acc is None else acc + pv
        o_ref[0, :, vcols] = (acc[:, :V_HEAD] * (1.0 / acc[:, V_HEAD:V_HEAD + 1])).astype(o_ref.dtype)


def _attn_body(q_ref, k_ref, v_ref, o_ref, *, tq, tk, heads):
    i = pl.program_id(2)
    for n in range(k_ref.shape[1] // tq):
        pl.when(i == n)(functools.partial(_attn_tile, q_ref, k_ref, v_ref, o_ref,
                                          tq=tq, tk=tk, heads=heads, n_blocks=n + 1))


def _attention(q, k, v):
    b, s, _ = q.shape
    tq, tk, hg = min(ATTN_Q, s), min(ATTN_K, s), ATTN_HEADS
    assert s % tq == 0 and tk % tq == 0 and MLA_HEADS % hg == 0
    return pl.pallas_call(
        functools.partial(_attn_body, tq=tq, tk=tk, heads=hg),
        grid=(b, MLA_HEADS // hg, s // tq),
        in_specs=[pl.BlockSpec((1, tq, hg * LANES), lambda bi, g, i: (bi, i, g)),
                  pl.BlockSpec((1, s, hg * LANES), lambda bi, g, i: (bi, 0, g), pipeline_mode=pl.Buffered(1)),
                  pl.BlockSpec((1, s, hg * LANES), lambda bi, g, i: (bi, 0, g), pipeline_mode=pl.Buffered(1))],
        out_specs=pl.BlockSpec((1, tq, hg * V_HEAD), lambda bi, g, i: (bi, i, g)),
        out_shape=jax.ShapeDtypeStruct((b, s, MLA_HEADS * V_HEAD), BF16),
        compiler_params=_params(3),
        name="attention",
    )(q, k, v)


def _out_body(a_ref, p_ref, halo_ref, h_ref, pw_ref, ps_ref, woa_ref, wop_ref, o_ref, *, tm):
    i = pl.program_id(1)
    p = p_ref[0]
    halo = jnp.where(i > 0, halo_ref[0], 0.0)
    ext = jnp.concatenate([halo, p], axis=0)
    t = i * tm + lax.broadcasted_iota(jnp.int32, (tm, 1), 0)
    mixed = []
    for g, w in enumerate(POOL_WINDOWS):
        sl = slice(g * POOL_GROUP_DIM, (g + 1) * POOL_GROUP_DIM)
        run, span = ext[:, sl], 1
        while span < w:
            run = run[span:] + run[:-span]
            span *= 2
        first = HALO - (w - 1)
        win = run[first:first + tm]
        count = jnp.minimum(t + 1, w).astype(F32)
        pooled = win / count - p[:, sl]
        mixed.append(jnp.dot(pooled.astype(BF16), pw_ref[g], preferred_element_type=F32))
    mixed = (jnp.concatenate(mixed, axis=-1) * ps_ref[...]).astype(BF16)
    o_ref[0] = (h_ref[0]
                + jnp.dot(a_ref[0], woa_ref[...], preferred_element_type=F32)
                + jnp.dot(mixed, wop_ref[...], preferred_element_type=F32))


def _out(attn, p, h, w):
    b, s, d = h.shape
    tm = min(OUT_TOKENS, s)
    wa, wp = attn.shape[-1], p.shape[-1]
    consts = (w["pool_w"], w["pool_scale"], w["w_out_a"], w["w_out_p"])
    tile = lambda width: pl.BlockSpec((1, tm, width), lambda bi, i: (bi, i, 0))
    halo_spec = pl.BlockSpec((1, HALO, wp), lambda bi, i: (bi, jnp.maximum(i * (tm // HALO) - 1, 0), 0))
    return pl.pallas_call(
        functools.partial(_out_body, tm=tm),
        grid=(b, s // tm),
        in_specs=[tile(wa), tile(wp), halo_spec, tile(d)] + [_const_spec(c.shape) for c in consts],
        out_specs=tile(d),
        out_shape=jax.ShapeDtypeStruct((b, s, d), F32),
        compiler_params=_params(2),
        name="even_out",
    )(attn, p, p, h, *consts)


def _gelu_tanh(x):
    return 0.5 * x * (1.0 + jnp.tanh(0.7978845608028654 * (x + 0.044715 * (x * x * x))))


def _sg_body(h_ref, mixg_ref, win_ref, sgn_ref, sgw_ref, sgb_ref, wout_ref, o_ref, gated_ref, *, tm):
    h = h_ref[...]
    d = h.shape[-1]
    hn = _rms(h, mixg_ref[...]).astype(BF16)
    uv = _gelu_tanh(jnp.dot(hn, win_ref[...], preferred_element_type=F32))
    u = uv[:, :d]
    vn = _rms(uv[:, d:], sgn_ref[...]).astype(BF16)
    gd = d // SG_GROUPS
    causal = (lax.broadcasted_iota(jnp.int32, (CHUNK, CHUNK), 1)
              <= lax.broadcasted_iota(jnp.int32, (CHUNK, CHUNK), 0))
    for g in range(SG_GROUPS):
        wg = jnp.where(causal, sgw_ref[g], 0.0).astype(BF16)
        bias = sgb_ref[:, g:g + 1]
        for c in range(tm // CHUNK):
            rows, cols = slice(c * CHUNK, (c + 1) * CHUNK), slice(g * gd, (g + 1) * gd)
            mixed = jnp.dot(wg, vn[rows, cols], preferred_element_type=F32) + bias
            gated_ref[rows, cols] = (u[rows, cols] * mixed).astype(BF16)
    o_ref[...] = h + jnp.dot(gated_ref[...], wout_ref[...], preferred_element_type=F32)


def _spatial_gating(x, w):
    n, d = x.shape
    tm = min(SG_TOKENS, n)
    consts = (w["mix_g"], w["w_in"], w["sg_norm"], w["sg_w"], w["sg_bt"], w["w_out"])
    return pl.pallas_call(
        functools.partial(_sg_body, tm=tm),
        grid=(n // tm,),
        in_specs=[pl.BlockSpec((tm, d), lambda i: (i, 0))] + [_const_spec(c.shape) for c in consts],
        out_specs=pl.BlockSpec((tm, d), lambda i: (i, 0)),
        out_shape=jax.ShapeDtypeStruct((n, d), F32),
        scratch_shapes=[pltpu.VMEM((tm, d), BF16)],
        compiler_params=_params(1),
        name="spatial_gating",
    )(x, *consts)


def _pad_heads(w, real, lead=0):
    k = w.shape[0]
    w = w.reshape(k, MLA_HEADS, real)
    w = jnp.pad(w, ((0, 0), (0, 0), (lead, LANES - lead - real)))
    return w.reshape(k, MLA_HEADS * LANES)


def _lane_row(values, lead):
    return jnp.pad(values.astype(F32), (lead, LANES - lead - values.shape[0])).reshape(1, LANES)


def _swap_halves(rope):
    half = QK_ROPE // 2
    return jnp.concatenate([rope[..., half:], rope[..., :half]], axis=-1)


def _signed_swapped_gain(g):
    half = QK_ROPE // 2
    rope = g[QK_NOPE:]
    return _lane_row(jnp.concatenate([-rope[half:], rope[:half]]), QK_NOPE)


def _prep_even(mix_g, w_in, qa_g, kva_g, w_uq, w_ukv, q_g, k_g, pool_w, pool_scale, w_out):
    d = w_in.shape[0]
    n_lat = Q_LORA + KV_LORA
    rope_pad = ((0, 0), (QK_NOPE, LANES - QK_HEAD))
    k_pe_cols = w_in[:, n_lat:n_lat + QK_ROPE]
    w_in_p = jnp.concatenate([w_in[:, :n_lat], jnp.pad(k_pe_cols, rope_pad),
                              jnp.pad(_swap_halves(k_pe_cols), rope_pad), w_in[:, n_lat + QK_ROPE:]], axis=1)
    uq = w_uq.reshape(Q_LORA, MLA_HEADS, QK_HEAD)
    uq_swap = jnp.pad(_swap_halves(uq[:, :, QK_NOPE:]), ((0, 0), (0, 0), (QK_NOPE, LANES - QK_HEAD)))
    ukv = w_ukv.reshape(KV_LORA, MLA_HEADS, QK_NOPE + V_HEAD)
    w_k = _pad_heads(ukv[:, :, :QK_NOPE].reshape(KV_LORA, -1), QK_NOPE)
    w_v = _pad_heads(ukv[:, :, QK_NOPE:].reshape(KV_LORA, -1), V_HEAD)
    inv_freq = ROPE_THETA ** (-jnp.arange(0, QK_ROPE, 2, dtype=F32) / QK_ROPE)
    attn_w = MLA_HEADS * V_HEAD
    q_gain = q_g * (QK_HEAD ** -0.5 * LOG2_E)
    return {
        "mix_g": mix_g.reshape(1, d),
        "w_in": w_in_p.astype(BF16),
        "qa_g": qa_g.reshape(1, Q_LORA),
        "kva_g": kva_g.reshape(1, KV_LORA),
        "w_uq": jnp.concatenate([_pad_heads(w_uq, QK_HEAD),
                                 uq_swap.reshape(Q_LORA, MLA_HEADS * LANES)], axis=1).astype(BF16),
        "w_ukv": jnp.concatenate([w_k, w_v], axis=1).astype(BF16),
        "q_g": _lane_row(q_gain, 0),
        "q_gs": _signed_swapped_gain(q_gain),
        "k_g": _lane_row(k_g, 0),
        "k_gs": _signed_swapped_gain(k_g),
        "invf": _lane_row(jnp.concatenate([inv_freq, inv_freq]), QK_NOPE),
        "v_one": jnp.tile(_lane_row(jnp.ones((1,), F32), V_HEAD), (1, MLA_HEADS)),
        "pool_w": pool_w.astype(BF16),
        "pool_scale": pool_scale.reshape(1, -1),
        "w_out_a": w_out[:attn_w].astype(BF16),
        "w_out_p": w_out[attn_w:].astype(BF16),
    }


def _prep_odd(mix_g, w_in, sg_norm, sg_w, sg_b, w_out):
    d = w_in.shape[0]
    return {
        "mix_g": mix_g.reshape(1, d),
        "w_in": w_in.astype(BF16),
        "sg_norm": sg_norm.reshape(1, -1),
        "sg_w": sg_w,
        "sg_bt": sg_b.T,
        "w_out": w_out.astype(BF16),
    }


def kernel(x, positions, ffn_norm, ffn_w_gate, ffn_w_up, ffn_w_down, mix_norm, even_w_in, q_a_norm, kv_a_norm, w_uq, w_ukv, q_norm, k_norm, pool_w, pool_scale, even_w_out, odd_w_in, sg_norm, sg_w, sg_b, odd_w_out):
    b, s, d = x.shape
    depth = ffn_norm.shape[0]
    pos = positions.reshape(b, s, 1)
    h = x.reshape(b * s, d)
    wg, wu, wd = ffn_w_gate.astype(BF16), ffn_w_up.astype(BF16), ffn_w_down.astype(BF16)
    fg = ffn_norm.reshape(depth * 2, 1, d)
    for layer in range(depth):
        i = layer // 2
        h = _ffn(h, fg, wg, wu, wd, layer, 0)
        if layer % 2 == 0:
            w = _prep_even(mix_norm[layer], even_w_in[i], q_a_norm[i], kv_a_norm[i], w_uq[i], w_ukv[i],
                           q_norm[i], k_norm[i], pool_w[i], pool_scale[i], even_w_out[i])
            h3 = h.reshape(b, s, d)
            q, k, v, p = _proj(h3, pos, w)
            attn = _attention(q, k, v)
            h = _out(attn, p, h3, w).reshape(b * s, d)
        else:
            w = _prep_odd(mix_norm[layer], odd_w_in[i], sg_norm[i], sg_w[i], sg_b[i], odd_w_out[i])
            h = _spatial_gating(h, w)
        h = _ffn(h, fg, wg, wu, wd, layer, 1)
    return h.reshape(b, s, d)
```

```python
import functools

import jax
import jax.numpy as jnp
from jax import lax
from jax.experimental import pallas as pl
from jax.experimental.pallas import tpu as pltpu

F32 = jnp.float32
BF16 = jnp.bfloat16

EPS = 1e-6
MLA_HEADS = 8
Q_LORA = 256
KV_LORA = 128
QK_NOPE = 64
QK_ROPE = 32
V_HEAD = 64
QK_HEAD = QK_NOPE + QK_ROPE
ROPE_THETA = 10000.0
LOG2_E = 1.4426950408889634
POOL_WINDOWS = (2, 4, 8, 16)
POOL_GROUP_DIM = 128
CHUNK = 128
SG_GROUPS = 4

LANES = 128
HALO = 16
VMEM_LIMIT_BYTES = 56 * 1024 * 1024

FFN_TOKENS = 512
FFN_CHUNK = 256
PROJ_TOKENS = 512
ATTN_Q = 512
ATTN_K = 2048
ATTN_HEADS = 2
ATTN_ROW_SPLIT = 2
OUT_TOKENS = 512
SG_TOKENS = 512


def _params(n_grid, flags=None):
    return pltpu.CompilerParams(dimension_semantics=("arbitrary",) * n_grid,
                                vmem_limit_bytes=VMEM_LIMIT_BYTES, flags=flags)


def _const_spec(shape):
    zeros = (0,) * len(shape)
    return pl.BlockSpec(shape, lambda *_: zeros, pipeline_mode=pl.Buffered(1))


def _rms(x, g):
    return x * lax.rsqrt(jnp.mean(x * x, axis=-1, keepdims=True) + EPS) * g


def _ffn_body(x_ref, g_ref, wg_ref, wu_ref, wd_ref, o_ref, *, tf):
    x = x_ref[...]
    xn = _rms(x, g_ref[...]).astype(BF16)
    acc = None
    for j in range(wg_ref.shape[1] // tf):
        cols = slice(j * tf, (j + 1) * tf)
        g = jnp.dot(xn, wg_ref[:, cols], preferred_element_type=F32)
        u = jnp.dot(xn, wu_ref[:, cols], preferred_element_type=F32)
        h = (g * jax.nn.sigmoid(g) * u).astype(BF16)
        d = jnp.dot(h, wd_ref[cols, :], preferred_element_type=F32)
        acc = d if acc is None else acc + d
    o_ref[...] = x + 0.5 * acc


def _ffn(x, g, wg, wu, wd, layer, which):
    n, d = x.shape
    f = wg.shape[-1]
    assert f % FFN_CHUNK == 0
    tm = min(FFN_TOKENS, n)
    pick = lambda rows, cols: pl.BlockSpec((None, None, rows, cols), lambda i: (layer, which, 0, 0),
                                           pipeline_mode=pl.Buffered(1))
    return pl.pallas_call(
        functools.partial(_ffn_body, tf=FFN_CHUNK),
        grid=(n // tm,),
        in_specs=[pl.BlockSpec((tm, d), lambda i: (i, 0)),
                  pl.BlockSpec((None, 1, d), lambda i: (2 * layer + which, 0, 0), pipeline_mode=pl.Buffered(1)),
                  pick(d, f), pick(d, f), pick(f, d)],
        out_specs=pl.BlockSpec((tm, d), lambda i: (i, 0)),
        out_shape=jax.ShapeDtypeStruct((n, d), F32),
        compiler_params=_params(1),
        name="ffn",
    )(x, g, wg, wu, wd)


def _head_inv_rms(xh):
    return lax.rsqrt(jnp.sum(xh * xh, axis=-1, keepdims=True) * (1.0 / QK_HEAD) + EPS)


def _proj_body(h_ref, pos_ref, mixg_ref, win_ref, qag_ref, kvag_ref, wuq_ref, wukv_ref,
               qg_ref, qgs_ref, kg_ref, kgs_ref, invf_ref, vone_ref, q_ref, k_ref, v_ref, p_ref):
    hq = MLA_HEADS * LANES
    hn = _rms(h_ref[0], mixg_ref[...]).astype(BF16)
    proj = jnp.dot(hn, win_ref[...], preferred_element_type=F32)
    n_lat = Q_LORA + KV_LORA
    c_q = proj[:, :Q_LORA]
    c_kv = proj[:, Q_LORA:n_lat]
    k_pe = proj[:, n_lat:n_lat + LANES]
    k_pe_swap = proj[:, n_lat + LANES:n_lat + 2 * LANES]
    p_ref[0] = proj[:, n_lat + 2 * LANES:]

    q = jnp.dot(_rms(c_q, qag_ref[...]).astype(BF16), wuq_ref[...], preferred_element_type=F32)
    kv = jnp.dot(_rms(c_kv, kvag_ref[...]).astype(BF16), wukv_ref[...], preferred_element_type=F32)

    ang = pos_ref[0].astype(F32) * invf_ref[...]
    cos, sin = jnp.cos(ang), jnp.sin(ang)
    q_cos, q_sin = qg_ref[...] * cos, qgs_ref[...] * sin
    k_cos, k_sin = kg_ref[...] * cos, kgs_ref[...] * sin
    k_rot = k_pe_swap * k_sin
    for hd in range(MLA_HEADS):
        sl = slice(hd * LANES, (hd + 1) * LANES)
        qh = q[:, sl]
        qs = q[:, hq + hd * LANES:hq + (hd + 1) * LANES]
        q_ref[0, :, sl] = ((qh * q_cos + qs * q_sin) * _head_inv_rms(qh)).astype(BF16)
        kh = kv[:, sl] + k_pe
        k_ref[0, :, sl] = ((kh * k_cos + k_rot) * _head_inv_rms(kh)).astype(BF16)
    v_ref[0] = (kv[:, hq:] + vone_ref[...]).astype(BF16)


def _proj(h, pos, w):
    b, s, d = h.shape
    tm = min(PROJ_TOKENS, s)
    consts = (w["mix_g"], w["w_in"], w["qa_g"], w["kva_g"], w["w_uq"], w["w_ukv"],
              w["q_g"], w["q_gs"], w["k_g"], w["k_gs"], w["invf"], w["v_one"])
    tile = lambda width: pl.BlockSpec((1, tm, width), lambda bi, i: (bi, i, 0))
    hq = MLA_HEADS * LANES
    return pl.pallas_call(
        _proj_body,
        grid=(b, s // tm),
        in_specs=[tile(d), tile(1)] + [_const_spec(c.shape) for c in consts],
        out_specs=[tile(hq), tile(hq), tile(hq), tile(len(POOL_WINDOWS) * POOL_GROUP_DIM)],
        out_shape=[jax.ShapeDtypeStruct((b, s, hq), BF16),
                   jax.ShapeDtypeStruct((b, s, hq), BF16),
                   jax.ShapeDtypeStruct((b, s, hq), BF16),
                   jax.ShapeDtypeStruct((b, s, len(POOL_WINDOWS) * POOL_GROUP_DIM), F32)],
        compiler_params=_params(2),
        name="even_proj",
    )(h, pos, *consts)


def _attn_tile(q_ref, k_ref, v_ref, o_ref, *, tq, tk, heads, n_blocks):
    rows_per = tq // ATTN_ROW_SPLIT
    row = lax.broadcasted_iota(jnp.int32, (rows_per, rows_per), 0)
    col = lax.broadcasted_iota(jnp.int32, (rows_per, rows_per), 1)
    for hh in range(heads):
        lanes = slice(hh * LANES, (hh + 1) * LANES)
        vcols = slice(hh * V_HEAD, (hh + 1) * V_HEAD)
        for r in range(ATTN_ROW_SPLIT):
            rows = slice(r * rows_per, (r + 1) * rows_per)
            width = (n_blocks - 1) * tq + (r + 1) * rows_per
            diag = width - rows_per
            bounds = [(c, min(c + tk, diag)) for c in range(0, diag, tk)] + [(diag, width)]
            q = q_ref[0, rows, lanes]
            scores = [lax.dot_general(q, k_ref[0, lo:hi, lanes], (((1,), (1,)), ((), ())),
                                      preferred_element_type=F32) for lo, hi in bounds]
            scores[-1] = jnp.where(col <= row, scores[-1], -jnp.inf)
            m = functools.reduce(jnp.maximum, [jnp.max(s, axis=-1, keepdims=True) for s in scores])
            acc = None
            for s, (lo, hi) in zip(scores, bounds):
                p = jnp.exp2(s - m).astype(BF16)
                pv = jnp.dot(p, v_ref[0, lo:hi, lanes], preferred_element_type=F32)
                acc = pv if acc is None else acc + pv
            o_ref[0, rows, vcols] = (acc[:, :V_HEAD] * (1.0 / acc[:, V_HEAD:V_HEAD + 1])).astype(o_ref.dtype)


def _attn_body(q_ref, k_ref, v_ref, o_ref, *, tq, tk, heads):
    i = pl.program_id(2)
    for n in range(k_ref.shape[1] // tq):
        pl.when(i == n)(functools.partial(_attn_tile, q_ref, k_ref, v_ref, o_ref,
                                          tq=tq, tk=tk, heads=heads, n_blocks=n + 1))


def _attention(q, k, v):
    b, s, _ = q.shape
    tq, tk, hg = min(ATTN_Q, s), min(ATTN_K, s), ATTN_HEADS
    assert s % tq == 0 and tk % tq == 0 and MLA_HEADS % hg == 0
    return pl.pallas_call(
        functools.partial(_attn_body, tq=tq, tk=tk, heads=hg),
        grid=(b, MLA_HEADS // hg, s // tq),
        in_specs=[pl.BlockSpec((1, tq, hg * LANES), lambda bi, g, i: (bi, i, g)),
                  pl.BlockSpec((1, s, hg * LANES), lambda bi, g, i: (bi, 0, g)),
                  pl.BlockSpec((1, s, hg * LANES), lambda bi, g, i: (bi, 0, g))],
        out_specs=pl.BlockSpec((1, tq, hg * V_HEAD), lambda bi, g, i: (bi, i, g)),
        out_shape=jax.ShapeDtypeStruct((b, s, MLA_HEADS * V_HEAD), BF16),
        compiler_params=_params(3),
        name="attention",
    )(q, k, v)


def _out_body(a_ref, p_ref, halo_ref, h_ref, pw_ref, ps_ref, woa_ref, wop_ref, o_ref, *, tm):
    i = pl.program_id(1)
    p = p_ref[0]
    halo = jnp.where(i > 0, halo_ref[0], 0.0)
    ext = jnp.concatenate([halo, p], axis=0)
    t = i * tm + lax.broadcasted_iota(jnp.int32, (tm, 1), 0)
    mixed = []
    for g, w in enumerate(POOL_WINDOWS):
        sl = slice(g * POOL_GROUP_DIM, (g + 1) * POOL_GROUP_DIM)
        run, span = ext[:, sl], 1
        while span < w:
            run = run[span:] + run[:-span]
            span *= 2
        first = HALO - (w - 1)
        win = run[first:first + tm]
        count = jnp.minimum(t + 1, w).astype(F32)
        pooled = win / count - p[:, sl]
        mixed.append(jnp.dot(pooled.astype(BF16), pw_ref[g], preferred_element_type=F32))
    mixed = (jnp.concatenate(mixed, axis=-1) * ps_ref[...]).astype(BF16)
    o_ref[0] = (h_ref[0]
                + jnp.dot(a_ref[0], woa_ref[...], preferred_element_type=F32)
                + jnp.dot(mixed, wop_ref[...], preferred_element_type=F32))


def _out(attn, p, h, w):
    b, s, d = h.shape
    tm = min(OUT_TOKENS, s)
    wa, wp = attn.shape[-1], p.shape[-1]
    consts = (w["pool_w"], w["pool_scale"], w["w_out_a"], w["w_out_p"])
    tile = lambda width: pl.BlockSpec((1, tm, width), lambda bi, i: (bi, i, 0))
    halo_spec = pl.BlockSpec((1, HALO, wp), lambda bi, i: (bi, jnp.maximum(i * (tm // HALO) - 1, 0), 0))
    return pl.pallas_call(
        functools.partial(_out_body, tm=tm),
        grid=(b, s // tm),
        in_specs=[tile(wa), tile(wp), halo_spec, tile(d)] + [_const_spec(c.shape) for c in consts],
        out_specs=tile(d),
        out_shape=jax.ShapeDtypeStruct((b, s, d), F32),
        compiler_params=_params(2),
        name="even_out",
    )(attn, p, p, h, *consts)


def _gelu_tanh(x):
    return 0.5 * x * (1.0 + jnp.tanh(0.7978845608028654 * (x + 0.044715 * (x * x * x))))


def _sg_body(h_ref, mixg_ref, win_ref, sgn_ref, sgw_ref, sgb_ref, wout_ref, o_ref, gated_ref, *, tm):
    h = h_ref[...]
    d = h.shape[-1]
    hn = _rms(h, mixg_ref[...]).astype(BF16)
    uv = _gelu_tanh(jnp.dot(hn, win_ref[...], preferred_element_type=F32))
    u = uv[:, :d]
    vn = _rms(uv[:, d:], sgn_ref[...]).astype(BF16)
    gd = d // SG_GROUPS
    causal = (lax.broadcasted_iota(jnp.int32, (CHUNK, CHUNK), 1)
              <= lax.broadcasted_iota(jnp.int32, (CHUNK, CHUNK), 0))
    for g in range(SG_GROUPS):
        wg = jnp.where(causal, sgw_ref[g], 0.0).astype(BF16)
        bias = sgb_ref[:, g:g + 1]
        for c in range(tm // CHUNK):
            rows, cols = slice(c * CHUNK, (c + 1) * CHUNK), slice(g * gd, (g + 1) * gd)
            mixed = jnp.dot(wg, vn[rows, cols], preferred_element_type=F32) + bias
            gated_ref[rows, cols] = (u[rows, cols] * mixed).astype(BF16)
    o_ref[...] = h + jnp.dot(gated_ref[...], wout_ref[...], preferred_element_type=F32)


def _spatial_gating(x, w):
    n, d = x.shape
    tm = min(SG_TOKENS, n)
    consts = (w["mix_g"], w["w_in"], w["sg_norm"], w["sg_w"], w["sg_bt"], w["w_out"])
    return pl.pallas_call(
        functools.partial(_sg_body, tm=tm),
        grid=(n // tm,),
        in_specs=[pl.BlockSpec((tm, d), lambda i: (i, 0))] + [_const_spec(c.shape) for c in consts],
        out_specs=pl.BlockSpec((tm, d), lambda i: (i, 0)),
        out_shape=jax.ShapeDtypeStruct((n, d), F32),
        scratch_shapes=[pltpu.VMEM((tm, d), BF16)],
        compiler_params=_params(1),
        name="spatial_gating",
    )(x, *consts)


def _pad_heads(w, real, lead=0):
    k = w.shape[0]
    w = w.reshape(k, MLA_HEADS, real)
    w = jnp.pad(w, ((0, 0), (0, 0), (lead, LANES - lead - real)))
    return w.reshape(k, MLA_HEADS * LANES)


def _lane_row(values, lead):
    return jnp.pad(values.astype(F32), (lead, LANES - lead - values.shape[0])).reshape(1, LANES)


def _swap_halves(rope):
    half = QK_ROPE // 2
    return jnp.concatenate([rope[..., half:], rope[..., :half]], axis=-1)


def _signed_swapped_gain(g):
    half = QK_ROPE // 2
    rope = g[QK_NOPE:]
    return _lane_row(jnp.concatenate([-rope[half:], rope[:half]]), QK_NOPE)


def _prep_even(mix_g, w_in, qa_g, kva_g, w_uq, w_ukv, q_g, k_g, pool_w, pool_scale, w_out):
    d = w_in.shape[0]
    n_lat = Q_LORA + KV_LORA
    rope_pad = ((0, 0), (QK_NOPE, LANES - QK_HEAD))
    k_pe_cols = w_in[:, n_lat:n_lat + QK_ROPE]
    w_in_p = jnp.concatenate([w_in[:, :n_lat], jnp.pad(k_pe_cols, rope_pad),
                              jnp.pad(_swap_halves(k_pe_cols), rope_pad), w_in[:, n_lat + QK_ROPE:]], axis=1)
    uq = w_uq.reshape(Q_LORA, MLA_HEADS, QK_HEAD)
    uq_swap = jnp.pad(_swap_halves(uq[:, :, QK_NOPE:]), ((0, 0), (0, 0), (QK_NOPE, LANES - QK_HEAD)))
    ukv = w_ukv.reshape(KV_LORA, MLA_HEADS, QK_NOPE + V_HEAD)
    w_k = _pad_heads(ukv[:, :, :QK_NOPE].reshape(KV_LORA, -1), QK_NOPE)
    w_v = _pad_heads(ukv[:, :, QK_NOPE:].reshape(KV_LORA, -1), V_HEAD)
    inv_freq = ROPE_THETA ** (-jnp.arange(0, QK_ROPE, 2, dtype=F32) / QK_ROPE)
    attn_w = MLA_HEADS * V_HEAD
    q_gain = q_g * (QK_HEAD ** -0.5 * LOG2_E)
    return {
        "mix_g": mix_g.reshape(1, d),
        "w_in": w_in_p.astype(BF16),
        "qa_g": qa_g.reshape(1, Q_LORA),
        "kva_g": kva_g.reshape(1, KV_LORA),
        "w_uq": jnp.concatenate([_pad_heads(w_uq, QK_HEAD),
                                 uq_swap.reshape(Q_LORA, MLA_HEADS * LANES)], axis=1).astype(BF16),
        "w_ukv": jnp.concatenate([w_k, w_v], axis=1).astype(BF16),
        "q_g": _lane_row(q_gain, 0),
        "q_gs": _signed_swapped_gain(q_gain),
        "k_g": _lane_row(k_g, 0),
        "k_gs": _signed_swapped_gain(k_g),
        "invf": _lane_row(jnp.concatenate([inv_freq, inv_freq]), QK_NOPE),
        "v_one": jnp.tile(_lane_row(jnp.ones((1,), F32), V_HEAD), (1, MLA_HEADS)),
        "pool_w": pool_w.astype(BF16),
        "pool_scale": pool_scale.reshape(1, -1),
        "w_out_a": w_out[:attn_w].astype(BF16),
        "w_out_p": w_out[attn_w:].astype(BF16),
    }


def _prep_odd(mix_g, w_in, sg_norm, sg_w, sg_b, w_out):
    d = w_in.shape[0]
    return {
        "mix_g": mix_g.reshape(1, d),
        "w_in": w_in.astype(BF16),
        "sg_norm": sg_norm.reshape(1, -1),
        "sg_w": sg_w,
        "sg_bt": sg_b.T,
        "w_out": w_out.astype(BF16),
    }


def kernel(x, positions, ffn_norm, ffn_w_gate, ffn_w_up, ffn_w_down, mix_norm, even_w_in, q_a_norm, kv_a_norm, w_uq, w_ukv, q_norm, k_norm, pool_w, pool_scale, even_w_out, odd_w_in, sg_norm, sg_w, sg_b, odd_w_out):
    b, s, d = x.shape
    depth = ffn_norm.shape[0]
    pos = positions.reshape(b, s, 1)
    h = x.reshape(b * s, d)
    wg, wu, wd = ffn_w_gate.astype(BF16), ffn_w_up.astype(BF16), ffn_w_down.astype(BF16)
    fg = ffn_norm.reshape(depth * 2, 1, d)
    for layer in range(depth):
        i = layer // 2
        h = _ffn(h, fg, wg, wu, wd, layer, 0)
        if layer % 2 == 0:
            w = _prep_even(mix_norm[layer], even_w_in[i], q_a_norm[i], kv_a_norm[i], w_uq[i], w_ukv[i],
                           q_norm[i], k_norm[i], pool_w[i], pool_scale[i], even_w_out[i])
            h3 = h.reshape(b, s, d)
            q, k, v, p = _proj(h3, pos, w)
            attn = _attention(q, k, v)
            h = _out(attn, p, h3, w).reshape(b * s, d)
        else:
            w = _prep_odd(mix_norm[layer], odd_w_in[i], sg_norm[i], sg_w[i], sg_b[i], odd_w_out[i])
            h = _spatial_gating(h, w)
        h = _ffn(h, fg, wg, wu, wd, layer, 1)
    return h.reshape(b, s, d)
```

```python
import functools

import jax
import jax.numpy as jnp
from jax import lax
from jax.experimental import pallas as pl
from jax.experimental.pallas import tpu as pltpu

F32 = jnp.float32
BF16 = jnp.bfloat16

EPS = 1e-6
MLA_HEADS = 8
Q_LORA = 256
KV_LORA = 128
QK_NOPE = 64
QK_ROPE = 32
V_HEAD = 64
QK_HEAD = QK_NOPE + QK_ROPE
ROPE_THETA = 10000.0
LOG2_E = 1.4426950408889634
POOL_WINDOWS = (2, 4, 8, 16)
POOL_GROUP_DIM = 128
CHUNK = 128
SG_GROUPS = 4

LANES = 128
HALO = 16
VMEM_LIMIT_BYTES = 56 * 1024 * 1024

FFN_TOKENS = 512
FFN_CHUNK = 256
PROJ_TOKENS = 512
ATTN_Q = 512
ATTN_K = 2048
ATTN_HEADS = 2
ATTN_ROW_SPLIT = 2
OUT_TOKENS = 512
SG_TOKENS = 512


def _params(n_grid, flags=None):
    return pltpu.CompilerParams(dimension_semantics=("arbitrary",) * n_grid,
                                vmem_limit_bytes=VMEM_LIMIT_BYTES, flags=flags)


def _const_spec(shape):
    zeros = (0,) * len(shape)
    return pl.BlockSpec(shape, lambda *_: zeros, pipeline_mode=pl.Buffered(1))


def _rms(x, g):
    return x * lax.rsqrt(jnp.mean(x * x, axis=-1, keepdims=True) + EPS) * g


def _half_swiglu(x, g_ref, wg_ref, wu_ref, wd_ref, tf):
    xn = _rms(x, g_ref[...]).astype(BF16)
    acc = None
    for j in range(wg_ref.shape[1] // tf):
        cols = slice(j * tf, (j + 1) * tf)
        g = jnp.dot(xn, wg_ref[:, cols], preferred_element_type=F32)
        u = jnp.dot(xn, wu_ref[:, cols], preferred_element_type=F32)
        h = (g * jax.nn.sigmoid(g) * u).astype(BF16)
        d = jnp.dot(h, wd_ref[cols, :], preferred_element_type=F32)
        acc = d if acc is None else acc + d
    return x + 0.5 * acc


def _ffn_body(x_ref, *refs, tf):
    x = x_ref[...]
    for k in range(0, len(refs) - 1, 4):
        x = _half_swiglu(x, *refs[k:k + 4], tf)
    refs[-1][...] = x


def _ffn(x, g, wg, wu, wd, *sets):
    n, d = x.shape
    f = wg.shape[-1]
    assert f % FFN_CHUNK == 0
    tm = min(FFN_TOKENS, n)
    specs, args = [], []
    for layer, which in sets:
        pick = lambda rows, cols, l=layer, w=which: pl.BlockSpec(
            (None, None, rows, cols), lambda i: (l, w, 0, 0), pipeline_mode=pl.Buffered(1))
        specs += [pl.BlockSpec((None, 1, d), lambda i, l=layer, w=which: (2 * l + w, 0, 0),
                               pipeline_mode=pl.Buffered(1)),
                  pick(d, f), pick(d, f), pick(f, d)]
        args += [g, wg, wu, wd]
    return pl.pallas_call(
        functools.partial(_ffn_body, tf=FFN_CHUNK),
        grid=(n // tm,),
        in_specs=[pl.BlockSpec((tm, d), lambda i: (i, 0))] + specs,
        out_specs=pl.BlockSpec((tm, d), lambda i: (i, 0)),
        out_shape=jax.ShapeDtypeStruct((n, d), F32),
        compiler_params=_params(1),
        name="ffn",
    )(x, *args)


def _head_inv_rms(xh):
    return lax.rsqrt(jnp.sum(xh * xh, axis=-1, keepdims=True) * (1.0 / QK_HEAD) + EPS)


def _proj_body(h_ref, pos_ref, mixg_ref, win_ref, qag_ref, kvag_ref, wuq_ref, wukv_ref,
               qg_ref, qgs_ref, kg_ref, kgs_ref, invf_ref, vone_ref, q_ref, k_ref, v_ref, p_ref):
    hq = MLA_HEADS * LANES
    hn = _rms(h_ref[0], mixg_ref[...]).astype(BF16)
    proj = jnp.dot(hn, win_ref[...], preferred_element_type=F32)
    n_lat = Q_LORA + KV_LORA
    c_q = proj[:, :Q_LORA]
    c_kv = proj[:, Q_LORA:n_lat]
    k_pe = proj[:, n_lat:n_lat + LANES]
    k_pe_swap = proj[:, n_lat + LANES:n_lat + 2 * LANES]
    p_ref[0] = proj[:, n_lat + 2 * LANES:]

    q = jnp.dot(_rms(c_q, qag_ref[...]).astype(BF16), wuq_ref[...], preferred_element_type=F32)
    kv = jnp.dot(_rms(c_kv, kvag_ref[...]).astype(BF16), wukv_ref[...], preferred_element_type=F32)

    ang = pos_ref[0].astype(F32) * invf_ref[...]
    cos, sin = jnp.cos(ang), jnp.sin(ang)
    q_cos, q_sin = qg_ref[...] * cos, qgs_ref[...] * sin
    k_cos, k_sin = kg_ref[...] * cos, kgs_ref[...] * sin
    k_rot = k_pe_swap * k_sin
    for hd in range(MLA_HEADS):
        sl = slice(hd * LANES, (hd + 1) * LANES)
        qh = q[:, sl]
        qs = q[:, hq + hd * LANES:hq + (hd + 1) * LANES]
        q_ref[0, :, sl] = ((qh * q_cos + qs * q_sin) * _head_inv_rms(qh)).astype(BF16)
        kh = kv[:, sl] + k_pe
        k_ref[0, :, sl] = ((kh * k_cos + k_rot) * _head_inv_rms(kh)).astype(BF16)
    v_ref[0] = (kv[:, hq:] + vone_ref[...]).astype(BF16)


def _proj(h, pos, w):
    b, s, d = h.shape
    tm = min(PROJ_TOKENS, s)
    consts = (w["mix_g"], w["w_in"], w["qa_g"], w["kva_g"], w["w_uq"], w["w_ukv"],
              w["q_g"], w["q_gs"], w["k_g"], w["k_gs"], w["invf"], w["v_one"])
    tile = lambda width: pl.BlockSpec((1, tm, width), lambda bi, i: (bi, i, 0))
    hq = MLA_HEADS * LANES
    return pl.pallas_call(
        _proj_body,
        grid=(b, s // tm),
        in_specs=[tile(d), tile(1)] + [_const_spec(c.shape) for c in consts],
        out_specs=[tile(hq), tile(hq), tile(hq), tile(len(POOL_WINDOWS) * POOL_GROUP_DIM)],
        out_shape=[jax.ShapeDtypeStruct((b, s, hq), BF16),
                   jax.ShapeDtypeStruct((b, s, hq), BF16),
                   jax.ShapeDtypeStruct((b, s, hq), BF16),
                   jax.ShapeDtypeStruct((b, s, len(POOL_WINDOWS) * POOL_GROUP_DIM), F32)],
        compiler_params=_params(2),
        name="even_proj",
    )(h, pos, *consts)


def _attn_tile(q_ref, k_ref, v_ref, o_ref, *, tq, tk, heads, n_blocks):
    rows_per = tq // ATTN_ROW_SPLIT
    row = lax.broadcasted_iota(jnp.int32, (rows_per, rows_per), 0)
    col = lax.broadcasted_iota(jnp.int32, (rows_per, rows_per), 1)
    for hh in range(heads):
        lanes = slice(hh * LANES, (hh + 1) * LANES)
        vcols = slice(hh * V_HEAD, (hh + 1) * V_HEAD)
        for r in range(ATTN_ROW_SPLIT):
            rows = slice(r * rows_per, (r + 1) * rows_per)
            width = (n_blocks - 1) * tq + (r + 1) * rows_per
            diag = width - rows_per
            bounds = [(c, min(c + tk, diag)) for c in range(0, diag, tk)] + [(diag, width)]
            q = q_ref[0, rows, lanes]
            scores = [lax.dot_general(q, k_ref[0, lo:hi, lanes], (((1,), (1,)), ((), ())),
                                      preferred_element_type=F32) for lo, hi in bounds]
            scores[-1] = jnp.where(col <= row, scores[-1], -jnp.inf)
            m = functools.reduce(jnp.maximum, [jnp.max(s, axis=-1, keepdims=True) for s in scores])
            acc = None
            for s, (lo, hi) in zip(scores, bounds):
                p = jnp.exp2(s - m).astype(BF16)
                pv = jnp.dot(p, v_ref[0, lo:hi, lanes], preferred_element_type=F32)
                acc = pv if acc is None else acc + pv
            o_ref[0, rows, vcols] = (acc[:, :V_HEAD] * (1.0 / acc[:, V_HEAD:V_HEAD + 1])).astype(o_ref.dtype)


def _attn_body(q_ref, k_ref, v_ref, o_ref, *, tq, tk, heads):
    i = pl.program_id(2)
    for n in range(k_ref.shape[1] // tq):
        pl.when(i == n)(functools.partial(_attn_tile, q_ref, k_ref, v_ref, o_ref,
                                          tq=tq, tk=tk, heads=heads, n_blocks=n + 1))


def _attention(q, k, v):
    b, s, _ = q.shape
    tq, tk, hg = min(ATTN_Q, s), min(ATTN_K, s), ATTN_HEADS
    assert s % tq == 0 and tk % tq == 0 and MLA_HEADS % hg == 0
    return pl.pallas_call(
        functools.partial(_attn_body, tq=tq, tk=tk, heads=hg),
        grid=(b, MLA_HEADS // hg, s // tq),
        in_specs=[pl.BlockSpec((1, tq, hg * LANES), lambda bi, g, i: (bi, i, g)),
                  pl.BlockSpec((1, s, hg * LANES), lambda bi, g, i: (bi, 0, g)),
                  pl.BlockSpec((1, s, hg * LANES), lambda bi, g, i: (bi, 0, g))],
        out_specs=pl.BlockSpec((1, tq, hg * V_HEAD), lambda bi, g, i: (bi, i, g)),
        out_shape=jax.ShapeDtypeStruct((b, s, MLA_HEADS * V_HEAD), BF16),
        compiler_params=_params(3),
        name="attention",
    )(q, k, v)


def _out_body(a_ref, p_ref, halo_ref, h_ref, pw_ref, ps_ref, woa_ref, wop_ref, o_ref, *, tm):
    i = pl.program_id(1)
    p = p_ref[0]
    halo = jnp.where(i > 0, halo_ref[0], 0.0)
    ext = jnp.concatenate([halo, p], axis=0)
    t = i * tm + lax.broadcasted_iota(jnp.int32, (tm, 1), 0)
    mixed = []
    for g, w in enumerate(POOL_WINDOWS):
        sl = slice(g * POOL_GROUP_DIM, (g + 1) * POOL_GROUP_DIM)
        run, span = ext[:, sl], 1
        while span < w:
            run = run[span:] + run[:-span]
            span *= 2
        first = HALO - (w - 1)
        win = run[first:first + tm]
        count = jnp.minimum(t + 1, w).astype(F32)
        pooled = win / count - p[:, sl]
        mixed.append(jnp.dot(pooled.astype(BF16), pw_ref[g], preferred_element_type=F32))
    mixed = (jnp.concatenate(mixed, axis=-1) * ps_ref[...]).astype(BF16)
    o_ref[0] = (h_ref[0]
                + jnp.dot(a_ref[0], woa_ref[...], preferred_element_type=F32)
                + jnp.dot(mixed, wop_ref[...], preferred_element_type=F32))


def _out(attn, p, h, w):
    b, s, d = h.shape
    tm = min(OUT_TOKENS, s)
    wa, wp = attn.shape[-1], p.shape[-1]
    consts = (w["pool_w"], w["pool_scale"], w["w_out_a"], w["w_out_p"])
    tile = lambda width: pl.BlockSpec((1, tm, width), lambda bi, i: (bi, i, 0))
    halo_spec = pl.BlockSpec((1, HALO, wp), lambda bi, i: (bi, jnp.maximum(i * (tm // HALO) - 1, 0), 0))
    return pl.pallas_call(
        functools.partial(_out_body, tm=tm),
        grid=(b, s // tm),
        in_specs=[tile(wa), tile(wp), halo_spec, tile(d)] + [_const_spec(c.shape) for c in consts],
        out_specs=tile(d),
        out_shape=jax.ShapeDtypeStruct((b, s, d), F32),
        compiler_params=_params(2),
        name="even_out",
    )(attn, p, p, h, *consts)


def _gelu_tanh(x):
    return 0.5 * x * (1.0 + jnp.tanh(0.7978845608028654 * (x + 0.044715 * (x * x * x))))


def _sg_body(h_ref, mixg_ref, win_ref, sgn_ref, sgw_ref, sgb_ref, wout_ref, o_ref, gated_ref, *, tm):
    h = h_ref[...]
    d = h.shape[-1]
    hn = _rms(h, mixg_ref[...]).astype(BF16)
    uv = _gelu_tanh(jnp.dot(hn, win_ref[...], preferred_element_type=F32))
    u = uv[:, :d]
    vn = _rms(uv[:, d:], sgn_ref[...]).astype(BF16)
    gd = d // SG_GROUPS
    causal = (lax.broadcasted_iota(jnp.int32, (CHUNK, CHUNK), 1)
              <= lax.broadcasted_iota(jnp.int32, (CHUNK, CHUNK), 0))
    for g in range(SG_GROUPS):
        wg = jnp.where(causal, sgw_ref[g], 0.0).astype(BF16)
        bias = sgb_ref[:, g:g + 1]
        for c in range(tm // CHUNK):
            rows, cols = slice(c * CHUNK, (c + 1) * CHUNK), slice(g * gd, (g + 1) * gd)
            mixed = jnp.dot(wg, vn[rows, cols], preferred_element_type=F32) + bias
            gated_ref[rows, cols] = (u[rows, cols] * mixed).astype(BF16)
    o_ref[...] = h + jnp.dot(gated_ref[...], wout_ref[...], preferred_element_type=F32)


def _spatial_gating(x, w):
    n, d = x.shape
    tm = min(SG_TOKENS, n)
    consts = (w["mix_g"], w["w_in"], w["sg_norm"], w["sg_w"], w["sg_bt"], w["w_out"])
    return pl.pallas_call(
        functools.partial(_sg_body, tm=tm),
        grid=(n // tm,),
        in_specs=[pl.BlockSpec((tm, d), lambda i: (i, 0))] + [_const_spec(c.shape) for c in consts],
        out_specs=pl.BlockSpec((tm, d), lambda i: (i, 0)),
        out_shape=jax.ShapeDtypeStruct((n, d), F32),
        scratch_shapes=[pltpu.VMEM((tm, d), BF16)],
        compiler_params=_params(1),
        name="spatial_gating",
    )(x, *consts)


def _pad_heads(w, real, lead=0):
    k = w.shape[0]
    w = w.reshape(k, MLA_HEADS, real)
    w = jnp.pad(w, ((0, 0), (0, 0), (lead, LANES - lead - real)))
    return w.reshape(k, MLA_HEADS * LANES)


def _lane_row(values, lead):
    return jnp.pad(values.astype(F32), (lead, LANES - lead - values.shape[0])).reshape(1, LANES)


def _swap_halves(rope):
    half = QK_ROPE // 2
    return jnp.concatenate([rope[..., half:], rope[..., :half]], axis=-1)


def _signed_swapped_gain(g):
    half = QK_ROPE // 2
    rope = g[QK_NOPE:]
    return _lane_row(jnp.concatenate([-rope[half:], rope[:half]]), QK_NOPE)


def _prep_even(mix_g, w_in, qa_g, kva_g, w_uq, w_ukv, q_g, k_g, pool_w, pool_scale, w_out):
    d = w_in.shape[0]
    n_lat = Q_LORA + KV_LORA
    rope_pad = ((0, 0), (QK_NOPE, LANES - QK_HEAD))
    k_pe_cols = w_in[:, n_lat:n_lat + QK_ROPE]
    w_in_p = jnp.concatenate([w_in[:, :n_lat], jnp.pad(k_pe_cols, rope_pad),
                              jnp.pad(_swap_halves(k_pe_cols), rope_pad), w_in[:, n_lat + QK_ROPE:]], axis=1)
    uq = w_uq.reshape(Q_LORA, MLA_HEADS, QK_HEAD)
    uq_swap = jnp.pad(_swap_halves(uq[:, :, QK_NOPE:]), ((0, 0), (0, 0), (QK_NOPE, LANES - QK_HEAD)))
    ukv = w_ukv.reshape(KV_LORA, MLA_HEADS, QK_NOPE + V_HEAD)
    w_k = _pad_heads(ukv[:, :, :QK_NOPE].reshape(KV_LORA, -1), QK_NOPE)
    w_v = _pad_heads(ukv[:, :, QK_NOPE:].reshape(KV_LORA, -1), V_HEAD)
    inv_freq = ROPE_THETA ** (-jnp.arange(0, QK_ROPE, 2, dtype=F32) / QK_ROPE)
    attn_w = MLA_HEADS * V_HEAD
    q_gain = q_g * (QK_HEAD ** -0.5 * LOG2_E)
    return {
        "mix_g": mix_g.reshape(1, d),
        "w_in": w_in_p.astype(BF16),
        "qa_g": qa_g.reshape(1, Q_LORA),
        "kva_g": kva_g.reshape(1, KV_LORA),
        "w_uq": jnp.concatenate([_pad_heads(w_uq, QK_HEAD),
                                 uq_swap.reshape(Q_LORA, MLA_HEADS * LANES)], axis=1).astype(BF16),
        "w_ukv": jnp.concatenate([w_k, w_v], axis=1).astype(BF16),
        "q_g": _lane_row(q_gain, 0),
        "q_gs": _signed_swapped_gain(q_gain),
        "k_g": _lane_row(k_g, 0),
        "k_gs": _signed_swapped_gain(k_g),
        "invf": _lane_row(jnp.concatenate([inv_freq, inv_freq]), QK_NOPE),
        "v_one": jnp.tile(_lane_row(jnp.ones((1,), F32), V_HEAD), (1, MLA_HEADS)),
        "pool_w": pool_w.astype(BF16),
        "pool_scale": pool_scale.reshape(1, -1),
        "w_out_a": w_out[:attn_w].astype(BF16),
        "w_out_p": w_out[attn_w:].astype(BF16),
    }


def _prep_odd(mix_g, w_in, sg_norm, sg_w, sg_b, w_out):
    d = w_in.shape[0]
    return {
        "mix_g": mix_g.reshape(1, d),
        "w_in": w_in.astype(BF16),
        "sg_norm": sg_norm.reshape(1, -1),
        "sg_w": sg_w,
        "sg_bt": sg_b.T,
        "w_out": w_out.astype(BF16),
    }


def kernel(x, positions, ffn_norm, ffn_w_gate, ffn_w_up, ffn_w_down, mix_norm, even_w_in, q_a_norm, kv_a_norm, w_uq, w_ukv, q_norm, k_norm, pool_w, pool_scale, even_w_out, odd_w_in, sg_norm, sg_w, sg_b, odd_w_out):
    b, s, d = x.shape
    depth = ffn_norm.shape[0]
    pos = positions.reshape(b, s, 1)
    h = x.reshape(b * s, d)
    wg, wu, wd = ffn_w_gate.astype(BF16), ffn_w_up.astype(BF16), ffn_w_down.astype(BF16)
    fg = ffn_norm.reshape(depth * 2, 1, d)
    h = _ffn(h, fg, wg, wu, wd, (0, 0))
    for layer in range(depth):
        i = layer // 2
        if layer % 2 == 0:
            w = _prep_even(mix_norm[layer], even_w_in[i], q_a_norm[i], kv_a_norm[i], w_uq[i], w_ukv[i],
                           q_norm[i], k_norm[i], pool_w[i], pool_scale[i], even_w_out[i])
            h3 = h.reshape(b, s, d)
            q, k, v, p = _proj(h3, pos, w)
            attn = _attention(q, k, v)
            h = _out(attn, p, h3, w).reshape(b * s, d)
        else:
            w = _prep_odd(mix_norm[layer], odd_w_in[i], sg_norm[i], sg_w[i], sg_b[i], odd_w_out[i])
            h = _spatial_gating(h, w)
        h = _ffn(h, fg, wg, wu, wd, *([(layer, 1)] + ([(layer + 1, 0)] if layer + 1 < depth else [])))
    return h.reshape(b, s, d)
```
